```python
import jax, jax.numpy as jnp
from jax import lax
import numpy as np


D_MODEL = 1024
BATCH = 32
SEQ = 2048
DEPTH = 4

MEM_LEN = 256
HEAD_DIM = 64
ROPE_THETA = 10000.0
NORM_EPS = 1e-6
NEG_INF = -1e30
D_FF = 2816
CONV_W = 512
CONV_K = 31
POOL_W = 512
POOL_GROUPS = 4
POOL_WINDOWS = (2, 4, 8, 16)
NSA_HEADS = 8
NSA_KV = 2
CMP_BLOCK = 32
CMP_STRIDE = 16
CMP_HIDDEN = 256
SLC_BLOCK = 64
N_SEL = 16
FORCE_BONUS = 1e4
NSA_WINDOW = 512
NSA_QCHUNK = 16
SWA_HEADS = 8
SWA_KV = 2
SWA_WINDOW = 128
Q_BLOCK = 128
X_HEADS = 4
X_HEAD_DIM = D_MODEL // X_HEADS
N_BRANCH = 4
BRANCH_W = 512
IN_SIZES = (2 * CONV_W, POOL_W, NSA_HEADS * HEAD_DIM, 6 * NSA_KV * HEAD_DIM, 3 * NSA_HEADS,
            SWA_HEADS * HEAD_DIM, 2 * SWA_KV * HEAD_DIM, N_BRANCH * D_MODEL)
D_IN = sum(IN_SIZES)

kernel_name = 'hybrid_gated_mixers_trunk'


def rms_norm(x, g):
    xf = x.astype(jnp.float32)
    y = xf * lax.rsqrt(jnp.mean(xf * xf, -1, keepdims=True) + NORM_EPS)
    return (y * g.astype(jnp.float32)).astype(x.dtype)


def layer_norm(x, g, b):
    xf = x.astype(jnp.float32)
    mu = jnp.mean(xf, -1, keepdims=True)
    var = jnp.mean(jnp.square(xf - mu), -1, keepdims=True)
    y = (xf - mu) * lax.rsqrt(var + NORM_EPS)
    return (y * g.astype(jnp.float32) + b.astype(jnp.float32)).astype(x.dtype)


def rope_tables(positions):
    inv = ROPE_THETA ** (-jnp.arange(0, HEAD_DIM, 2, dtype=jnp.float32) / HEAD_DIM)
    ang = positions.astype(jnp.float32)[..., None] * inv
    return jnp.cos(ang)[:, :, None, :], jnp.sin(ang)[:, :, None, :]


def apply_rope(x, cos, sin):
    half = x.shape[-1] // 2
    xf = x.astype(jnp.float32)
    x1, x2 = xf[..., :half], xf[..., half:]
    return jnp.concatenate([x1 * cos - x2 * sin, x2 * cos + x1 * sin], -1).astype(x.dtype)


def masked_softmax(s, mask):
    sm = jnp.where(mask, s, NEG_INF)
    e = jnp.exp(sm - jnp.max(sm, -1, keepdims=True)) * mask
    return e / jnp.maximum(jnp.sum(e, -1, keepdims=True), 1e-30)


def swiglu(h, w_in, w_out):
    a, b = jnp.split(h @ w_in, 2, axis=-1)
    return (jax.nn.silu(a) * b) @ w_out


def conv_module(u, conv_w, conv_b, ln_g, ln_b):
    a, g = jnp.split(u, 2, axis=-1)
    v = a * jax.nn.sigmoid(g)
    y = lax.conv_general_dilated(v, conv_w[:, None, :], window_strides=(1,),
                                 padding=[(CONV_K - 1, 0)],
                                 dimension_numbers=('NWC', 'WIO', 'NWC'),
                                 feature_group_count=CONV_W) + conv_b
    return jax.nn.silu(layer_norm(y, ln_g, ln_b))


def pool_mixer(v, w_pool, scale):
    B, S, C = v.shape
    cg_w = C // POOL_GROUPS
    vf = v.astype(jnp.float32)
    c = jnp.concatenate([jnp.zeros((B, 1, C), jnp.float32), jnp.cumsum(vf, axis=1)], axis=1)
    t = jnp.arange(S)
    outs = []
    for g, w in enumerate(POOL_WINDOWS):
        cg = c[..., g * cg_w:(g + 1) * cg_w]
        lag = jnp.pad(cg, ((0, 0), (w - 1, 0), (0, 0)))[:, :S]
        cnt = jnp.minimum(t + 1, w).astype(jnp.float32)[None, :, None]
        outs.append((cg[:, 1:] - lag) / cnt - vf[..., g * cg_w:(g + 1) * cg_w])
    d = jnp.stack(outs, axis=2).astype(v.dtype)
    y = jnp.einsum('bsgc,gcd->bsgd', d, w_pool).reshape(B, S, C)
    return y * scale


def banded_attention(q, k, v, window, sinks=None):
    B, S, H, dh = q.shape
    G = k.shape[2]
    R = H // G
    n_blk = S // Q_BLOCK
    L = Q_BLOCK + window
    pad = ((0, 0), (window, 0), (0, 0), (0, 0))
    kp = jnp.pad(k, pad)
    vp = jnp.pad(v, pad)
    qb = jnp.moveaxis(q.reshape(B, n_blk, Q_BLOCK, G, R, dh), 1, 0)
    scale = dh ** -0.5

    def block(args):
        i, qi = args
        start = i * Q_BLOCK
        ki = lax.dynamic_slice_in_dim(kp, start, L, axis=1)
        vi = lax.dynamic_slice_in_dim(vp, start, L, axis=1)
        qpos = start + jnp.arange(Q_BLOCK)
        kpos = start - window + jnp.arange(L)
        diff = qpos[:, None] - kpos[None, :]
        mask = (diff >= 0) & (diff < window) & (kpos[None, :] >= 0)
        s = jnp.einsum('bqgrd,bkgd->bgrqk', qi, ki).astype(jnp.float32) * scale
        s = jnp.where(mask, s, NEG_INF)
        if sinks is None:
            p = jax.nn.softmax(s, axis=-1)
        else:
            sk = sinks.astype(jnp.float32).reshape(G, R)[None, :, :, None, None]
            m = jnp.maximum(jnp.max(s, -1, keepdims=True), sk)
            e = jnp.exp(s - m)
            p = e / (jnp.sum(e, -1, keepdims=True) + jnp.exp(sk - m))
        return jnp.einsum('bgrqk,bkgd->bqgrd', p.astype(v.dtype), vi)

    o = lax.map(block, (jnp.arange(n_blk), qb))
    return jnp.moveaxis(o, 0, 1).reshape(B, S, H, dh)


def nsa_attention(q, k_c, v_c, k_s, v_s, k_w, v_w, gate_logits,
                  k_pos, k_w1, k_b1, k_w2, k_b2, v_pos, v_w1, v_b1, v_w2, v_b2):
    B, S, H, dh = q.shape
    G = k_c.shape[2]
    R = H // G
    scale = dh ** -0.5
    n_c = (S - CMP_BLOCK) // CMP_STRIDE + 1
    cmp_start = np.arange(n_c) * CMP_STRIDE
    cmp_idx = cmp_start[:, None] + np.arange(CMP_BLOCK)[None, :]
    cmp_end = jnp.asarray(cmp_start + CMP_BLOCK - 1, jnp.int32)

    def compress(t, pos, w1, b1, w2, b2):
        blk = t[:, cmp_idx] + pos[None, None, :, None, :]
        blk = jnp.moveaxis(blk, 3, 2).reshape(B, n_c, G, CMP_BLOCK * dh)
        return jax.nn.gelu(blk @ w1 + b1) @ w2 + b2

    k_cmp = compress(k_c, k_pos, k_w1, k_b1, k_w2, k_b2)
    v_cmp = compress(v_c, v_pos, v_w1, v_b1, v_w2, v_b2)
    n_s = S // SLC_BLOCK
    sel_start = np.arange(n_s) * SLC_BLOCK
    overlap = jnp.asarray(((cmp_start[:, None] <= sel_start[None, :] + SLC_BLOCK - 1) &
                           (cmp_start[:, None] + CMP_BLOCK - 1 >= sel_start[None, :])).astype(np.float32))
    k_top = min(N_SEL, n_s)
    kb = jnp.moveaxis(k_s.reshape(B, n_s, SLC_BLOCK, G, dh), 3, 1)
    vb = jnp.moveaxis(v_s.reshape(B, n_s, SLC_BLOCK, G, dh), 3, 1)
    b_idx = jnp.arange(B)[:, None, None, None]
    g_idx = jnp.arange(G)[None, :, None, None]
    blk_ids = jnp.arange(n_s)
    n_ch = S // NSA_QCHUNK
    qc = jnp.moveaxis(q.reshape(B, n_ch, NSA_QCHUNK, G, R, dh), 1, 0)

    def chunk(args):
        i, qi = args
        qpos = i * NSA_QCHUNK + jnp.arange(NSA_QCHUNK)
        s_c = jnp.einsum('bqgrd,bcgd->bgrqc', qi, k_cmp).astype(jnp.float32) * scale
        p_c = masked_softmax(s_c, cmp_end[None, :] <= qpos[:, None])
        o_c = jnp.einsum('bgrqc,bcgd->bqgrd', p_c.astype(v_cmp.dtype), v_cmp)
        imp = jnp.einsum('bgrqc,cj->bgqj', p_c, overlap)
        forced = (blk_ids[None, :] == 0) | (blk_ids[None, :] == (qpos // SLC_BLOCK)[:, None])
        causal = blk_ids[None, :] * SLC_BLOCK <= qpos[:, None]
        score = jnp.where(causal, imp + jnp.where(forced, FORCE_BONUS, 0.0), NEG_INF)
        _, sel = lax.top_k(score, k_top)
        ks = kb[b_idx, g_idx, sel]
        vs = vb[b_idx, g_idx, sel]
        spos = sel[..., None] * SLC_BLOCK + jnp.arange(SLC_BLOCK)
        smask = (spos <= qpos[None, None, :, None, None])[:, :, None]
        s_s = jnp.einsum('bqgrd,bgqkld->bgrqkl', qi, ks).astype(jnp.float32) * scale
        s_s = jnp.where(smask, s_s, NEG_INF)
        p_s = jax.nn.softmax(s_s.reshape(s_s.shape[:4] + (-1,)), axis=-1).reshape(s_s.shape)
        o_s = jnp.einsum('bgrqkl,bgqkld->bqgrd', p_s.astype(vs.dtype), vs)
        return o_c, o_s

    o_c, o_s = lax.map(chunk, (jnp.arange(n_ch), qc))
    o_c = jnp.moveaxis(o_c, 0, 1).reshape(B, S, H, dh)
    o_s = jnp.moveaxis(o_s, 0, 1).reshape(B, S, H, dh)
    o_w = banded_attention(q, k_w, v_w, NSA_WINDOW)
    g = jax.nn.sigmoid(gate_logits.astype(jnp.float32)).reshape(B, S, H, 3).astype(q.dtype)
    o = g[..., 0:1] * o_c + g[..., 1:2] * o_s + g[..., 2:3] * o_w
    return o.reshape(B, S, H * dh)


def cross_attention(h, mem_n, w_q, w_kv, w_o):
    B, S, D = h.shape
    M = mem_n.shape[1]
    q = (h @ w_q).reshape(B, S, X_HEADS, X_HEAD_DIM)
    kv = (mem_n @ w_kv).reshape(B, M, 2, X_HEADS, X_HEAD_DIM)
    s = jnp.einsum('bqhd,bkhd->bhqk', q, kv[:, :, 0]).astype(jnp.float32) * (X_HEAD_DIM ** -0.5)
    p = jax.nn.softmax(s, axis=-1)
    o = jnp.einsum('bhqk,bkhd->bqhd', p.astype(h.dtype), kv[:, :, 1]).reshape(B, S, D)
    return o @ w_o


def setup_inputs(seed: int = 0) -> dict:
    key = jax.random.key(seed)
    keys = iter(jax.random.split(key, 48))
    f32 = jnp.float32
    L = DEPTH
    D = D_MODEL

    def dense(shape, fan_in):
        return jax.random.normal(next(keys), shape, f32) * (fan_in ** -0.5)

    def gain(shape):
        return 1.0 + 0.05 * jax.random.normal(next(keys), shape, f32)

    def small(shape, s=0.02):
        return s * jax.random.normal(next(keys), shape, f32)

    x = jax.random.normal(next(keys), (BATCH, SEQ, D), f32)
    mem = jax.random.normal(next(keys), (BATCH, MEM_LEN, D), f32)
    offset = jax.random.randint(next(keys), (BATCH, 1), 0, 4096, dtype=jnp.int32)
    positions = offset + jnp.arange(SEQ, dtype=jnp.int32)[None, :]
    cg = POOL_W // POOL_GROUPS
    return {
        'x': x, 'mem': mem, 'positions': positions,
        'ffn1_pre_g': gain((L, D)), 'ffn1_w_in': dense((L, D, 2 * D_FF), D),
        'ffn1_w_out': dense((L, D_FF, D), D_FF), 'ffn1_post_g': gain((L, D)),
        'mix_pre_g': gain((L, D)), 'w_in': dense((L, D, D_IN), D),
        'conv_w': dense((L, CONV_K, CONV_W), CONV_K), 'conv_b': small((L, CONV_W)),
        'conv_ln_g': gain((L, CONV_W)), 'conv_ln_b': small((L, CONV_W)),
        'pool_w': dense((L, POOL_GROUPS, cg, cg), cg), 'pool_scale': gain((L, POOL_W)),
        'cmp_k_pos': small((L, CMP_BLOCK, HEAD_DIM), 0.1),
        'cmp_k_w1': dense((L, CMP_BLOCK * HEAD_DIM, CMP_HIDDEN), CMP_BLOCK * HEAD_DIM),
        'cmp_k_b1': small((L, CMP_HIDDEN)), 'cmp_k_w2': dense((L, CMP_HIDDEN, HEAD_DIM), CMP_HIDDEN),
        'cmp_k_b2': small((L, HEAD_DIM)),
        'cmp_v_pos': small((L, CMP_BLOCK, HEAD_DIM), 0.1),
        'cmp_v_w1': dense((L, CMP_BLOCK * HEAD_DIM, CMP_HIDDEN), CMP_BLOCK * HEAD_DIM),
        'cmp_v_b1': small((L, CMP_HIDDEN)), 'cmp_v_w2': dense((L, CMP_HIDDEN, HEAD_DIM), CMP_HIDDEN),
        'cmp_v_b2': small((L, HEAD_DIM)),
        'swa_sinks': jax.random.normal(next(keys), (L, SWA_HEADS), f32),
        'w_branch': dense((L, N_BRANCH, BRANCH_W, D), BRANCH_W), 'w_out': dense((L, D, D), D),
        'mix_post_g': gain((L, D)),
        'x_pre_g': gain((L, D)), 'mem_g': gain((L, D)), 'w_xq': dense((L, D, D), D),
        'w_xkv': dense((L, D, 2 * D), D), 'w_xo': dense((L, D, D), D), 'x_post_g': gain((L, D)),
        'ffn2_pre_g': gain((L, D)), 'ffn2_w_in': dense((L, D, 2 * D_FF), D),
        'ffn2_w_out': dense((L, D_FF, D), D_FF), 'ffn2_post_g': gain((L, D)),
    }


def reference(x, mem, positions,
              ffn1_pre_g, ffn1_w_in, ffn1_w_out, ffn1_post_g,
              mix_pre_g, w_in, conv_w, conv_b, conv_ln_g, conv_ln_b, pool_w, pool_scale,
              cmp_k_pos, cmp_k_w1, cmp_k_b1, cmp_k_w2, cmp_k_b2,
              cmp_v_pos, cmp_v_w1, cmp_v_b1, cmp_v_w2, cmp_v_b2,
              swa_sinks, w_branch, w_out, mix_post_g,
              x_pre_g, mem_g, w_xq, w_xkv, w_xo, x_post_g,
              ffn2_pre_g, ffn2_w_in, ffn2_w_out, ffn2_post_g):
    B, S, D = x.shape
    cos, sin = rope_tables(positions)
    splits = [int(o) for o in np.cumsum(IN_SIZES)[:-1]]
    for l in range(DEPTH):
        h = rms_norm(x, ffn1_pre_g[l])
        x = x + 0.5 * rms_norm(swiglu(h, ffn1_w_in[l], ffn1_w_out[l]), ffn1_post_g[l])
        h = rms_norm(x, mix_pre_g[l])
        u_conv, u_pool, q_nsa, kv_nsa, g_nsa, q_swa, kv_swa, u_gate = jnp.split(h @ w_in[l], splits, axis=-1)
        y_a = conv_module(u_conv, conv_w[l], conv_b[l], conv_ln_g[l], conv_ln_b[l])
        y_b = pool_mixer(u_pool, pool_w[l], pool_scale[l])
        q_n = apply_rope(q_nsa.reshape(B, S, NSA_HEADS, HEAD_DIM), cos, sin)
        kv_n = kv_nsa.reshape(B, S, 3, 2, NSA_KV, HEAD_DIM)
        k_n = apply_rope(kv_n[:, :, :, 0].reshape(B, S, 3 * NSA_KV, HEAD_DIM), cos, sin)
        k_n = k_n.reshape(B, S, 3, NSA_KV, HEAD_DIM)
        v_n = kv_n[:, :, :, 1]
        y_c = nsa_attention(q_n, k_n[:, :, 0], v_n[:, :, 0], k_n[:, :, 1], v_n[:, :, 1],
                            k_n[:, :, 2], v_n[:, :, 2], g_nsa,
                            cmp_k_pos[l], cmp_k_w1[l], cmp_k_b1[l], cmp_k_w2[l], cmp_k_b2[l],
                            cmp_v_pos[l], cmp_v_w1[l], cmp_v_b1[l], cmp_v_w2[l], cmp_v_b2[l])
        q_s = apply_rope(q_swa.reshape(B, S, SWA_HEADS, HEAD_DIM), cos, sin)
        kv_s = kv_swa.reshape(B, S, 2, SWA_KV, HEAD_DIM)
        k_s = apply_rope(kv_s[:, :, 0], cos, sin)
        y_d = banded_attention(q_s, k_s, kv_s[:, :, 1], SWA_WINDOW, swa_sinks[l]).reshape(B, S, BRANCH_W)
        gates = jax.nn.sigmoid(u_gate.reshape(B, S, N_BRANCH, D))
        branches = (y_a, y_b, y_c, y_d)
        merged = gates[:, :, 0] * (y_a @ w_branch[l, 0])
        for n in range(1, N_BRANCH):
            merged = merged + gates[:, :, n] * (branches[n] @ w_branch[l, n])
        x = x + rms_norm(merged @ w_out[l], mix_post_g[l])
        h = rms_norm(x, x_pre_g[l])
        mem_n = rms_norm(mem, mem_g[l])
        x = x + rms_norm(cross_attention(h, mem_n, w_xq[l], w_xkv[l], w_xo[l]), x_post_g[l])
        h = rms_norm(x, ffn2_pre_g[l])
        x = x + 0.5 * rms_norm(swiglu(h, ffn2_w_in[l], ffn2_w_out[l]), ffn2_post_g[l])
    return x
```

```python
import functools

import numpy as np
import jax
import jax.numpy as jnp
from jax import lax
from jax.experimental import pallas as pl
from jax.experimental.pallas import tpu as pltpu

F32 = jnp.float32
BF16 = jnp.bfloat16

D_MODEL = 1024
DEPTH = 4
HEAD_DIM = 64
ROPE_THETA = 10000.0
NORM_EPS = 1e-6
NEG_INF = -1e30
D_FF = 2816
CONV_W = 512
CONV_K = 31
POOL_W = 512
POOL_GROUPS = 4
POOL_WINDOWS = (2, 4, 8, 16)
NSA_HEADS = 8
NSA_KV = 2
CMP_BLOCK = 32
CMP_STRIDE = 16
CMP_HIDDEN = 256
SLC_BLOCK = 64
N_SEL = 16
FORCE_BONUS = 1e4
NSA_WINDOW = 512
SWA_HEADS = 8
SWA_KV = 2
SWA_WINDOW = 128
X_HEADS = 4
X_HEAD_DIM = D_MODEL // X_HEADS
N_BRANCH = 4
BRANCH_W = 512
IN_SIZES = (2 * CONV_W, POOL_W, NSA_HEADS * HEAD_DIM, 6 * NSA_KV * HEAD_DIM, 3 * NSA_HEADS,
            SWA_HEADS * HEAD_DIM, 2 * SWA_KV * HEAD_DIM, N_BRANCH * D_MODEL)
IN_OFFS = tuple(int(o) for o in np.cumsum((0,) + IN_SIZES))

LANES = 128
VMEM_LIMIT = 56 * 1024 * 1024
HEADS_PER_GROUP = NSA_HEADS // NSA_KV
Q_TILE = 128
SEL_KEY_TILE = 256
CONV_T = 128
CONV_HALO = 32
POOL_HALO = 16


def _params(*sem):
    return pltpu.CompilerParams(dimension_semantics=sem, vmem_limit_bytes=VMEM_LIMIT)


def _const_spec(shape):
    nd = len(shape)
    return pl.BlockSpec(shape, lambda *_: (0,) * nd, pipeline_mode=pl.Buffered(1))


def _rms(x, g):
    return x * lax.rsqrt(jnp.mean(x * x, -1, keepdims=True) + NORM_EPS) * g


def _sigmoid(x):
    return 1.0 / (1.0 + jnp.exp(-x))


def _dot(a, b):
    return jnp.dot(a, b, preferred_element_type=F32)


def _dot_nt(a, b):
    return lax.dot_general(a, b, (((1,), (1,)), ((), ())), preferred_element_type=F32)


def _split_dot(x, e):
    hi = x.astype(BF16)
    lo = (x - hi.astype(F32)).astype(BF16)
    return _dot(hi, e) + _dot(lo, e)


def _rope_kernel(pos_ref, inv_ref, sgn_ref, cos_ref, sin_ref):
    ang = pos_ref[...].astype(F32) * inv_ref[...]
    cos_ref[...] = jnp.cos(ang)
    sin_ref[...] = jnp.sin(ang) * sgn_ref[...]


def _rope_tables(positions):
    n = positions.size
    tm = 2048
    inv = ROPE_THETA ** (-jnp.arange(0, HEAD_DIM, 2, dtype=F32) / HEAD_DIM)
    inv = jnp.tile(inv, LANES // (HEAD_DIM // 2))[None, :]
    sgn = jnp.tile(jnp.concatenate([-jnp.ones(HEAD_DIM // 2, F32), jnp.ones(HEAD_DIM // 2, F32)]),
                   LANES // HEAD_DIM)[None, :]
    return pl.pallas_call(
        _rope_kernel,
        grid=(n // tm,),
        in_specs=[pl.BlockSpec((tm, 1), lambda i: (i, 0)),
                  _const_spec((1, LANES)), _const_spec((1, LANES))],
        out_specs=[pl.BlockSpec((tm, LANES), lambda i: (i, 0))] * 2,
        out_shape=[jax.ShapeDtypeStruct((n, LANES), F32)] * 2,
        compiler_params=_params("parallel"),
        name="rope_tables",
    )(positions.reshape(n, 1), inv, sgn)


def _ffn_kernel(x_ref, pre_ref, wa_ref, wb_ref, wo_ref, post_ref, o_ref):
    x = x_ref[...]
    h = _rms(x, pre_ref[...]).astype(BF16)
    a = _dot(h, wa_ref[...])
    b = _dot(h, wb_ref[...])
    t = (a * _sigmoid(a) * b).astype(BF16)
    y = _dot(t, wo_ref[...])
    o_ref[...] = x + 0.5 * _rms(y, post_ref[...])


def _ffn(x, pre_g, w_in, w_out, post_g, tm=512):
    n, d = x.shape
    f = w_out.shape[0]
    return pl.pallas_call(
        _ffn_kernel,
        grid=(n // tm,),
        in_specs=[pl.BlockSpec((tm, d), lambda i: (i, 0)),
                  _const_spec((1, d)),
                  pl.BlockSpec((d, f), lambda i: (0, 0), pipeline_mode=pl.Buffered(1)),
                  pl.BlockSpec((d, f), lambda i: (0, 1), pipeline_mode=pl.Buffered(1)),
                  _const_spec((f, d)),
                  _const_spec((1, d))],
        out_specs=pl.BlockSpec((tm, d), lambda i: (i, 0)),
        out_shape=jax.ShapeDtypeStruct((n, d), F32),
        compiler_params=_params("parallel"),
        name="ffn",
    )(x, pre_g[None, :], w_in, w_in, w_out, post_g[None, :])


def _inproj_kernel(x_ref, g_ref, cos_ref, sin_ref,
                   w_uc, w_up, w_qn, w_kn, w_gn, w_qs, w_ks, w_gt,
                   uc_ref, up_ref, qn_ref, kc_ref, vc_ref, ksw_ref, gn_ref, qs_ref, kvs_ref, gt_ref):
    h = _rms(x_ref[...], g_ref[...]).astype(BF16)
    cos_t = cos_ref[...]
    sin_t = sin_ref[...]
    lane = lax.broadcasted_iota(jnp.int32, (1, LANES), 1)
    first_half = (lane & (HEAD_DIM // 2)) == 0

    def rope(seg):
        rot = jnp.where(first_half, pltpu.roll(seg, LANES - HEAD_DIM // 2, 1),
                        pltpu.roll(seg, HEAD_DIM // 2, 1))
        return seg * cos_t + rot * sin_t

    def seg(v, j):
        return v[:, j * LANES:(j + 1) * LANES]

    q_scale = HEAD_DIM ** -0.5
    uc_ref[...] = _dot(h, w_uc[...]).astype(BF16)
    up_ref[...] = _dot(h, w_up[...]).astype(BF16)
    qn = _dot(h, w_qn[...])
    for p in range(NSA_HEADS):
        qn_ref[:, p * LANES:(p + 1) * LANES] = (rope(seg(qn, p)) * q_scale).astype(BF16)
    kn = _dot(h, w_kn[...])
    kc_ref[...] = rope(seg(kn, 0))
    vc_ref[...] = seg(kn, 1)
    ksw_ref[:, 0 * LANES:1 * LANES] = rope(seg(kn, 2)).astype(BF16)
    ksw_ref[:, 1 * LANES:2 * LANES] = seg(kn, 3).astype(BF16)
    ksw_ref[:, 2 * LANES:3 * LANES] = rope(seg(kn, 4)).astype(BF16)
    ksw_ref[:, 3 * LANES:4 * LANES] = seg(kn, 5).astype(BF16)
    gn_ref[...] = _dot(h, w_gn[...])
    qs = _dot(h, w_qs[...])
    for p in range(SWA_HEADS):
        qs_ref[:, p * LANES:(p + 1) * LANES] = (rope(seg(qs, p)) * q_scale).astype(BF16)
    ks = _dot(h, w_ks[...])
    kvs_ref[:, 0:LANES] = rope(seg(ks, 0)).astype(BF16)
    kvs_ref[:, LANES:2 * LANES] = seg(ks, 1).astype(BF16)
    gt_ref[...] = _dot(h, w_gt[...]).astype(BF16)


def _expand_q(wq, n_groups, per_group):
    d = wq.shape[0]
    w = wq.reshape(d, n_groups, per_group, HEAD_DIM)
    w = jnp.einsum('kgrd,gs->kgrsd', w, jnp.eye(n_groups, dtype=wq.dtype))
    return w.reshape(d, n_groups * per_group * n_groups * HEAD_DIM)


def _inproj_weights(w_in):
    o = IN_OFFS
    w_uc = w_in[:, o[0]:o[1]]
    w_up = w_in[:, o[1]:o[2]]
    w_qn = _expand_q(w_in[:, o[2]:o[3]], NSA_KV, HEADS_PER_GROUP)
    w_kn = w_in[:, o[3]:o[4]]
    w_gn = jnp.pad(w_in[:, o[4]:o[5]], ((0, 0), (0, LANES - 3 * NSA_HEADS)))
    w_qs = _expand_q(w_in[:, o[5]:o[6]], SWA_KV, SWA_HEADS // SWA_KV)
    w_ks = w_in[:, o[6]:o[7]]
    w_gt = w_in[:, o[7]:o[8]]
    return tuple(w.astype(BF16) for w in (w_uc, w_up, w_qn, w_kn, w_gn, w_qs, w_ks, w_gt))


def _inproj(x, g, cos_t, sin_t, weights, tm=256):
    n, d = x.shape
    widths = [(2 * CONV_W, BF16), (POOL_W, BF16), (NSA_HEADS * LANES, BF16), (LANES, F32), (LANES, F32),
              (4 * LANES, BF16), (LANES, F32), (SWA_HEADS * LANES, BF16), (2 * LANES, BF16),
              (N_BRANCH * D_MODEL, BF16)]
    return pl.pallas_call(
        _inproj_kernel,
        grid=(n // tm,),
        in_specs=[pl.BlockSpec((tm, d), lambda i: (i, 0)), _const_spec((1, d)),
                  pl.BlockSpec((tm, LANES), lambda i: (i, 0)), pl.BlockSpec((tm, LANES), lambda i: (i, 0))]
                 + [_const_spec(w.shape) for w in weights],
        out_specs=[pl.BlockSpec((tm, w), lambda i: (i, 0)) for w, _ in widths],
        out_shape=[jax.ShapeDtypeStruct((n, w), dt) for w, dt in widths],
        compiler_params=_params("parallel"),
        name="inproj",
    )(x, g[None, :], cos_t, sin_t, *weights)


def _convpool_kernel(uc_ref, up_ref, cw_ref, cb_ref, lng_ref, lnb_ref, pw_ref, psc_ref,
                     ya_ref, yb_ref, vbuf, pbuf, cbuf, *, seq):
    t_c = CONV_T
    vbuf[0:CONV_HALO, :] = jnp.zeros((CONV_HALO, CONV_W), F32)
    pbuf[0:POOL_HALO, :] = jnp.zeros((POOL_HALO, POOL_W), F32)

    def fill(c, carry):
        r0 = pl.multiple_of(c * t_c, t_c)
        u = uc_ref[0, pl.ds(r0, t_c), :].astype(F32)
        vbuf[pl.ds(CONV_HALO + r0, t_c), :] = u[:, :CONV_W] * _sigmoid(u[:, CONV_W:])
        pbuf[pl.ds(POOL_HALO + r0, t_c), :] = up_ref[0, pl.ds(r0, t_c), :].astype(F32)
        return carry

    lax.fori_loop(0, seq // t_c, fill, 0)

    cg = POOL_W // POOL_GROUPS

    def mix(c, carry):
        r0 = pl.multiple_of(c * t_c, t_c)
        for lt in range(CONV_W // LANES):
            ls = slice(lt * LANES, (lt + 1) * LANES)
            xext = vbuf[pl.ds(r0, t_c + CONV_HALO), ls]
            acc = jnp.broadcast_to(cb_ref[:, ls], (t_c, LANES))
            for r in range(8):
                rolled = xext if r == 0 else pltpu.roll(xext, r, 0)
                for a in range(CONV_HALO // 8):
                    shift = 8 * a + r
                    if shift > CONV_K - 1:
                        continue
                    k = CONV_K - 1 - shift
                    base = CONV_HALO - 8 * a
                    acc = acc + rolled[base:base + t_c, :] * cw_ref[k:k + 1, ls]
            cbuf[:, ls] = acc
        y = cbuf[...]
        mu = jnp.mean(y, -1, keepdims=True)
        yc = y - mu
        var = jnp.mean(yc * yc, -1, keepdims=True)
        z = yc * lax.rsqrt(var + NORM_EPS) * lng_ref[...] + lnb_ref[...]
        ya_ref[0, pl.ds(r0, t_c), :] = (z * _sigmoid(z)).astype(BF16)

        pext = pbuf[pl.ds(r0, t_c + POOL_HALO), :]
        e2 = pext + pltpu.roll(pext, 1, 0)
        e4 = e2[:, cg:] + pltpu.roll(e2[:, cg:], 2, 0)
        e8 = e4[:, cg:] + pltpu.roll(e4[:, cg:], 4, 0)
        e16 = e8[:, cg:] + pltpu.roll(e8[:, cg:], 8, 0)
        tpos = r0 + lax.broadcasted_iota(jnp.int32, (t_c, 1), 0)
        for g, (w, e) in enumerate(zip(POOL_WINDOWS, (e2, e4, e8, e16))):
            cnt = jnp.minimum(tpos + 1, w).astype(F32)
            v_g = pext[POOL_HALO:, g * cg:(g + 1) * cg]
            dlt = e[POOL_HALO:, :cg] / cnt - v_g
            yg = _dot(dlt.astype(BF16), pw_ref[g]) * psc_ref[:, g * cg:(g + 1) * cg]
            yb_ref[0, pl.ds(r0, t_c), g * cg:(g + 1) * cg] = yg.astype(BF16)
        return carry

    lax.fori_loop(0, seq // t_c, mix, 0)


def _convpool(uc, up, conv_w, conv_b, ln_g, ln_b, pool_w, pool_scale):
    b, s, _ = uc.shape
    return pl.pallas_call(
        functools.partial(_convpool_kernel, seq=s),
        grid=(b,),
        in_specs=[pl.BlockSpec((1, s, 2 * CONV_W), lambda i: (i, 0, 0)),
                  pl.BlockSpec((1, s, POOL_W), lambda i: (i, 0, 0)),
                  _const_spec((CONV_K, CONV_W)), _const_spec((1, CONV_W)),
                  _const_spec((1, CONV_W)), _const_spec((1, CONV_W)),
                  _const_spec(pool_w.shape), _const_spec((1, POOL_W))],
        out_specs=[pl.BlockSpec((1, s, CONV_W), lambda i: (i, 0, 0)),
                   pl.BlockSpec((1, s, POOL_W), lambda i: (i, 0, 0))],
        out_shape=[jax.ShapeDtypeStruct((b, s, CONV_W), BF16), jax.ShapeDtypeStruct((b, s, POOL_W), BF16)],
        scratch_shapes=[pltpu.VMEM((CONV_HALO + s, CONV_W), F32), pltpu.VMEM((POOL_HALO + s, POOL_W), F32),
                        pltpu.VMEM((CONV_T, CONV_W), F32)],
        compiler_params=_params("parallel"),
        name="convpool",
    )(uc, up, conv_w, conv_b[None, :], ln_g[None, :], ln_b[None, :], pool_w.astype(BF16), pool_scale[None, :])


def _gelu_tanh(x):
    c = np.float32(np.sqrt(2.0 / np.pi))
    return x * (0.5 * (1.0 + jnp.tanh(c * (x + 0.044715 * (x * x * x)))))


def _compress_kernel(kc_ref, vc_ref, pos_ref, w1_ref, b1_ref, w2_ref, b2_ref, ko_ref, vo_ref):
    n_half = kc_ref.shape[1]
    for t, (src, dst) in enumerate(((kc_ref, ko_ref), (vc_ref, vo_ref))):
        x = src[0]
        top = _dot((x + pos_ref[t, 0:1, :]).astype(BF16), w1_ref[t, 0])
        bot = _dot((x + pos_ref[t, 1:2, :]).astype(BF16), w1_ref[t, 1])
        h1 = top + pltpu.roll(bot, n_half - 1, 0) + b1_ref[t]
        out = _dot(_gelu_tanh(h1).astype(BF16), w2_ref[t]) + b2_ref[t]
        dst[0] = out.astype(BF16)


def _compress_weights(pos, w1, b1, w2, b2):
    half = CMP_BLOCK // 2
    eye = jnp.eye(NSA_KV, dtype=F32)
    pos_e = jnp.tile(pos.reshape(2, half, 1, HEAD_DIM), (1, 1, NSA_KV, 1)).reshape(2, half * NSA_KV * HEAD_DIM)
    w1r = w1.reshape(2, half, HEAD_DIM, CMP_HIDDEN)
    w1e = jnp.einsum('tldn,gh->tlgdhn', w1r, eye).reshape(2, half * NSA_KV * HEAD_DIM, NSA_KV * CMP_HIDDEN)
    b1e = jnp.tile(b1, NSA_KV)[None, :]
    w2e = jnp.einsum('nd,gh->gnhd', w2, eye).reshape(NSA_KV * CMP_HIDDEN, NSA_KV * HEAD_DIM)
    b2e = jnp.tile(b2, NSA_KV)[None, :]
    return pos_e, w1e.astype(BF16), b1e, w2e.astype(BF16), b2e


def _compress(kc, vc, kparams, vparams):
    b, nh, width = kc.shape
    stk = [jnp.stack([kp, vp]) for kp, vp in zip(kparams, vparams)]
    return pl.pallas_call(
        _compress_kernel,
        grid=(b,),
        in_specs=[pl.BlockSpec((1, nh, width), lambda i: (i, 0, 0))] * 2 + [_const_spec(a.shape) for a in stk],
        out_specs=[pl.BlockSpec((1, nh, LANES), lambda i: (i, 0, 0))] * 2,
        out_shape=[jax.ShapeDtypeStruct((b, nh, LANES), BF16)] * 2,
        compiler_params=_params("parallel"),
        name="nsa_compress",
    )(kc, vc, *stk)


def _stack_heads(q_all, g):
    return jnp.concatenate(
        [q_all[:, (HEADS_PER_GROUP * g + r) * LANES:(HEADS_PER_GROUP * g + r + 1) * LANES]
         for r in range(HEADS_PER_GROUP)], axis=0)


def _pair_groups(o_g0, o_g1, r, tq):
    lane = lax.broadcasted_iota(jnp.int32, (1, LANES), 1)
    return jnp.where(lane < HEAD_DIM, o_g0[r * tq:(r + 1) * tq], o_g1[r * tq:(r + 1) * tq])


def _pair_rows(w):
    d = w.shape[1]
    return w.reshape(NSA_KV, HEADS_PER_GROUP, HEAD_DIM, d).transpose(1, 0, 2, 3).reshape(-1, d)


def _nsa_kernel(q_ref, kc_ref, vc_ref, ks_ref, vs_ref, kw_ref, vw_ref, gl_ref, ovt_ref, esel_ref, eg_ref,
                o_ref, *, seq):
    tq = Q_TILE
    tk = SEL_KEY_TILE
    m_rows = HEADS_PER_GROUP * tq
    n_cmp = seq // CMP_STRIDE
    n_sel = seq // SLC_BLOCK
    k_top = min(N_SEL, n_sel)
    w_keys = NSA_WINDOW + tq
    q0 = pl.program_id(1) * tq
    q_all = q_ref[0]
    row = lax.broadcasted_iota(jnp.int32, (m_rows, 1), 0)
    qpos = q0 + (row & (tq - 1))
    kcm = kc_ref[0]
    vcm = vc_ref[0]
    cidx = lax.broadcasted_iota(jnp.int32, (1, n_cmp), 1)
    cmp_valid = ((cidx * CMP_STRIDE + (CMP_BLOCK - 1)) <= qpos) & (cidx < n_cmp - 1)
    jidx = lax.broadcasted_iota(jnp.int32, (n_sel, 1), 0)
    qpos_t = q0 + lax.broadcasted_iota(jnp.int32, (1, tq), 1)
    forced = (jidx == 0) | (jidx == jnp.right_shift(qpos_t, int(np.log2(SLC_BLOCK))))
    causal_blk = (jidx * SLC_BLOCK) <= qpos_t
    n_kt = (q0 + tq + tk - 1) // tk
    kstart = pl.multiple_of(jnp.maximum(q0 - NSA_WINDOW, 0), tq)
    wpos = kstart + lax.broadcasted_iota(jnp.int32, (1, w_keys), 1)
    wdiff = qpos - wpos
    win_mask = (wdiff >= 0) & (wdiff < NSA_WINDOW)

    o_cmp, o_sel, o_win = [], [], []
    for g in range(NSA_KV):
        qg = _stack_heads(q_all, g)

        s_c = _dot_nt(qg, kcm)
        sm = jnp.where(cmp_valid, s_c, NEG_INF)
        e = jnp.where(cmp_valid, jnp.exp(sm - jnp.max(sm, -1, keepdims=True)), 0.0)
        p_c = e / jnp.maximum(jnp.sum(e, -1, keepdims=True), 1e-30)
        o_cmp.append(_dot(p_c.astype(BF16), vcm))

        psum = p_c[0:tq]
        for r in range(1, HEADS_PER_GROUP):
            psum = psum + p_c[r * tq:(r + 1) * tq]
        p_hi = psum.astype(BF16)
        p_lo = (psum - p_hi.astype(F32)).astype(BF16)
        ovt = ovt_ref[...]
        imp_t = (_dot_nt(ovt, p_hi) + _dot_nt(ovt, p_lo))[0:n_sel]
        score = jnp.where(causal_blk, imp_t + jnp.where(forced, FORCE_BONUS, 0.0), NEG_INF)
        rank = jnp.zeros((n_sel, tq), jnp.int32)
        for i in range(n_sel):
            ri = score[i:i + 1, :]
            rank = rank + jnp.where(jidx > i, jnp.where(ri >= score, 1, 0), jnp.where(ri > score, 1, 0))
        bias_t = jnp.where(rank < k_top, 0.0, NEG_INF)
        bias_t = jnp.concatenate([bias_t, jnp.zeros((LANES - n_sel, tq), F32)], axis=0)
        bias = bias_t.T.astype(BF16)
        q_aug = jnp.concatenate([qg, jnp.concatenate([bias] * HEADS_PER_GROUP, axis=0)], axis=1)

        def sel_tile(kt, carry, diagonal):
            m_i, l_i, acc = carry
            k0 = pl.multiple_of(kt * tk, tk)
            k_aug = jnp.concatenate([ks_ref[0, pl.ds(k0, tk), :], esel_ref[pl.ds(k0, tk), :]], axis=1)
            s = _dot_nt(q_aug, k_aug)
            if diagonal:
                kpos = k0 + lax.broadcasted_iota(jnp.int32, (1, tk), 1)
                s = jnp.where(kpos <= qpos, s, NEG_INF)
            m_n = jnp.maximum(m_i, jnp.max(s, -1, keepdims=True))
            alpha = jnp.exp(m_i - m_n)
            p = jnp.exp(s - m_n)
            l_n = alpha * l_i + jnp.sum(p, -1, keepdims=True)
            acc_n = alpha * acc + _dot(p.astype(BF16), vs_ref[0, pl.ds(k0, tk), :])
            return m_n, l_n, acc_n

        init = (jnp.full((m_rows, 1), NEG_INF, F32), jnp.zeros((m_rows, 1), F32),
                jnp.zeros((m_rows, LANES), F32))
        carry = lax.fori_loop(0, n_kt - 1, functools.partial(sel_tile, diagonal=False), init)
        _, l_f, acc_f = sel_tile(n_kt - 1, carry, True)
        o_sel.append(acc_f / l_f)

        s_w = _dot_nt(qg, kw_ref[0, pl.ds(kstart, w_keys), :])
        s_w = jnp.where(win_mask, s_w, NEG_INF)
        e_w = jnp.exp(s_w - jnp.max(s_w, -1, keepdims=True))
        p_w = e_w / jnp.sum(e_w, -1, keepdims=True)
        o_win.append(_dot(p_w.astype(BF16), vw_ref[0, pl.ds(kstart, w_keys), :]))

    gates = _sigmoid(gl_ref[0])
    g_exp = [_split_dot(gates, eg_ref[j]) for j in range(3)]
    for r in range(HEADS_PER_GROUP):
        ls = slice(r * LANES, (r + 1) * LANES)
        y = (g_exp[0][:, ls] * _pair_groups(o_cmp[0], o_cmp[1], r, tq)
             + g_exp[1][:, ls] * _pair_groups(o_sel[0], o_sel[1], r, tq)
             + g_exp[2][:, ls] * _pair_groups(o_win[0], o_win[1], r, tq))
        o_ref[0, :, ls] = y.astype(BF16)


def _nsa_constants(seq):
    n_cmp = seq // CMP_STRIDE
    n_sel = seq // SLC_BLOCK
    cs = np.arange(n_cmp) * CMP_STRIDE
    ss = np.arange(n_sel) * SLC_BLOCK
    ov = (cs[:, None] <= ss[None, :] + SLC_BLOCK - 1) & (cs[:, None] + CMP_BLOCK - 1 >= ss[None, :])
    ov[n_cmp - 1, :] = False
    ovt = np.zeros((LANES, n_cmp), np.float32)
    ovt[:n_sel, :] = ov.T
    esel = np.zeros((seq, LANES), np.float32)
    esel[np.arange(seq), np.arange(seq) // SLC_BLOCK] = 1.0
    eg = np.zeros((3, LANES, NSA_HEADS * HEAD_DIM), np.float32)
    for g in range(NSA_KV):
        for r in range(HEADS_PER_GROUP):
            for j in range(3):
                c0 = r * LANES + g * HEAD_DIM
                eg[j, (g * HEADS_PER_GROUP + r) * 3 + j, c0:c0 + HEAD_DIM] = 1.0
    return jnp.asarray(ovt, BF16), jnp.asarray(esel, BF16), jnp.asarray(eg, BF16)


def _nsa(qn, kcm, vcm, ksw, gn, consts):
    b, s, _ = qn.shape
    assert s % SEL_KEY_TILE == 0 and s >= NSA_WINDOW + Q_TILE and s // SLC_BLOCK <= LANES
    n_cmp = s // CMP_STRIDE
    ovt, esel, eg = consts
    kv_spec = lambda j: pl.BlockSpec((1, s, LANES), lambda i, t: (i, 0, j))
    return pl.pallas_call(
        functools.partial(_nsa_kernel, seq=s),
        grid=(b, s // Q_TILE),
        in_specs=[pl.BlockSpec((1, Q_TILE, NSA_HEADS * LANES), lambda i, t: (i, t, 0)),
                  pl.BlockSpec((1, n_cmp, LANES), lambda i, t: (i, 0, 0)),
                  pl.BlockSpec((1, n_cmp, LANES), lambda i, t: (i, 0, 0)),
                  kv_spec(0), kv_spec(1), kv_spec(2), kv_spec(3),
                  pl.BlockSpec((1, Q_TILE, LANES), lambda i, t: (i, t, 0)),
                  _const_spec(ovt.shape), _const_spec(esel.shape), _const_spec(eg.shape)],
        out_specs=pl.BlockSpec((1, Q_TILE, NSA_HEADS * HEAD_DIM), lambda i, t: (i, t, 0)),
        out_shape=jax.ShapeDtypeStruct((b, s, NSA_HEADS * HEAD_DIM), BF16),
        compiler_params=_params("parallel", "arbitrary"),
        name="nsa_attention",
    )(qn, kcm, vcm, ksw, ksw, ksw, ksw, gn, ovt, esel, eg)


def _swa_kernel(sink_ref, q_ref, k_ref, v_ref, o_ref):
    tq = Q_TILE
    m_rows = HEADS_PER_GROUP * tq
    w_keys = SWA_WINDOW + tq
    q0 = pl.program_id(1) * tq
    q_all = q_ref[0]
    row = lax.broadcasted_iota(jnp.int32, (m_rows, 1), 0)
    qpos = q0 + (row & (tq - 1))
    kstart = pl.multiple_of(jnp.maximum(q0 - SWA_WINDOW, 0), tq)
    kpos = kstart + lax.broadcasted_iota(jnp.int32, (1, w_keys), 1)
    diff = qpos - kpos
    mask = (diff >= 0) & (diff < SWA_WINDOW)
    k_t = k_ref[0, pl.ds(kstart, w_keys), :]
    v_t = v_ref[0, pl.ds(kstart, w_keys), :]
    outs = []
    for g in range(SWA_KV):
        qg = _stack_heads(q_all, g)
        sink = jnp.concatenate(
            [jnp.full((tq, 1), sink_ref[HEADS_PER_GROUP * g + r], F32) for r in range(HEADS_PER_GROUP)], axis=0)
        s = jnp.where(mask, _dot_nt(qg, k_t), NEG_INF)
        m = jnp.maximum(jnp.max(s, -1, keepdims=True), sink)
        e = jnp.exp(s - m)
        p = e / (jnp.sum(e, -1, keepdims=True) + jnp.exp(sink - m))
        outs.append(_dot(p.astype(BF16), v_t))
    for r in range(HEADS_PER_GROUP):
        o_ref[0, :, r * LANES:(r + 1) * LANES] = _pair_groups(outs[0], outs[1], r, tq).astype(BF16)


def _swa(qs, kvs, sinks):
    b, s, _ = qs.shape
    assert s >= SWA_WINDOW + Q_TILE
    return pl.pallas_call(
        _swa_kernel,
        grid=(b, s // Q_TILE),
        in_specs=[pl.BlockSpec(memory_space=pltpu.SMEM),
                  pl.BlockSpec((1, Q_TILE, SWA_HEADS * LANES), lambda i, t: (i, t, 0)),
                  pl.BlockSpec((1, s, LANES), lambda i, t: (i, 0, 0)),
                  pl.BlockSpec((1, s, LANES), lambda i, t: (i, 0, 1))],
        out_specs=pl.BlockSpec((1, Q_TILE, SWA_HEADS * HEAD_DIM), lambda i, t: (i, t, 0)),
        out_shape=jax.ShapeDtypeStruct((b, s, SWA_HEADS * HEAD_DIM), BF16),
        compiler_params=_params("parallel", "arbitrary"),
        name="swa_attention",
    )(sinks, qs, kvs, kvs)


def _merge_kernel(x_ref, ya_ref, yb_ref, yc_ref, yd_ref, gt_ref, wb_ref, wo_ref, g_ref, o_ref):
    merged = None
    for n, y_ref in enumerate((ya_ref, yb_ref, yc_ref, yd_ref)):
        z = _dot(y_ref[...], wb_ref[n])
        gate = _sigmoid(gt_ref[:, n * D_MODEL:(n + 1) * D_MODEL].astype(F32))
        merged = gate * z if merged is None else merged + gate * z
    out = _dot(merged.astype(BF16), wo_ref[...])
    o_ref[...] = x_ref[...] + _rms(out, g_ref[...])


def _merge(x, ya, yb, yc, yd, gt, wb, wo, g, tm=512):
    n, d = x.shape
    row = lambda w: pl.BlockSpec((tm, w), lambda i: (i, 0))
    return pl.pallas_call(
        _merge_kernel,
        grid=(n // tm,),
        in_specs=[row(d), row(BRANCH_W), row(BRANCH_W), row(BRANCH_W), row(BRANCH_W), row(N_BRANCH * d),
                  _const_spec(wb.shape), _const_spec(wo.shape), _const_spec((1, d))],
        out_specs=row(d),
        out_shape=jax.ShapeDtypeStruct((n, d), F32),
        compiler_params=_params("parallel"),
        name="merge",
    )(x, ya, yb, yc, yd, gt, wb, wo, g[None, :])


def _memkv_kernel(mem_ref, g_ref, w_ref, o_ref):
    mn = _rms(mem_ref[0], g_ref[...]).astype(BF16)
    o_ref[0] = _dot(mn, w_ref[...]).astype(BF16)


def _memkv(mem, g, w_kv):
    b, m, d = mem.shape
    return pl.pallas_call(
        _memkv_kernel,
        grid=(b,),
        in_specs=[pl.BlockSpec((1, m, d), lambda i: (i, 0, 0)), _const_spec((1, d)), _const_spec(w_kv.shape)],
        out_specs=pl.BlockSpec((1, m, 2 * d), lambda i: (i, 0, 0)),
        out_shape=jax.ShapeDtypeStruct((b, m, 2 * d), BF16),
        compiler_params=_params("parallel"),
        name="mem_kv",
    )(mem, g[None, :], w_kv)


def _xattn_kernel(x_ref, pre_ref, wq_ref, kv_ref, wo_ref, post_ref, o_ref):
    x = x_ref[0]
    h = _rms(x, pre_ref[...]).astype(BF16)
    q = (_dot(h, wq_ref[...]) * (X_HEAD_DIM ** -0.5)).astype(BF16)
    heads = []
    for hd in range(X_HEADS):
        k_h = kv_ref[0, :, hd * X_HEAD_DIM:(hd + 1) * X_HEAD_DIM]
        v_h = kv_ref[0, :, D_MODEL + hd * X_HEAD_DIM:D_MODEL + (hd + 1) * X_HEAD_DIM]
        s = _dot_nt(q[:, hd * X_HEAD_DIM:(hd + 1) * X_HEAD_DIM], k_h)
        e = jnp.exp(s - jnp.max(s, -1, keepdims=True))
        p = e / jnp.sum(e, -1, keepdims=True)
        heads.append(_dot(p.astype(BF16), v_h).astype(BF16))
    o = jnp.concatenate(heads, axis=1)
    y = _dot(o, wo_ref[...])
    o_ref[0] = x + _rms(y, post_ref[...])


def _xattn(x, pre_g, w_q, kv, w_o, post_g, tm=512):
    b, s, d = x.shape
    m = kv.shape[1]
    return pl.pallas_call(
        _xattn_kernel,
        grid=(b, s // tm),
        in_specs=[pl.BlockSpec((1, tm, d), lambda i, t: (i, t, 0)), _const_spec((1, d)), _const_spec(w_q.shape),
                  pl.BlockSpec((1, m, 2 * d), lambda i, t: (i, 0, 0)), _const_spec(w_o.shape), _const_spec((1, d))],
        out_specs=pl.BlockSpec((1, tm, d), lambda i, t: (i, t, 0)),
        out_shape=jax.ShapeDtypeStruct((b, s, d), F32),
        compiler_params=_params("parallel", "arbitrary"),
        name="cross_attention",
    )(x, pre_g[None, :], w_q, kv, w_o, post_g[None, :])


def kernel(x, mem, positions, ffn1_pre_g, ffn1_w_in, ffn1_w_out, ffn1_post_g, mix_pre_g, w_in, conv_w, conv_b, conv_ln_g, conv_ln_b, pool_w, pool_scale, cmp_k_pos, cmp_k_w1, cmp_k_b1, cmp_k_w2, cmp_k_b2, cmp_v_pos, cmp_v_w1, cmp_v_b1, cmp_v_w2, cmp_v_b2, swa_sinks, w_branch, w_out, mix_post_g, x_pre_g, mem_g, w_xq, w_xkv, w_xo, x_post_g, ffn2_pre_g, ffn2_w_in, ffn2_w_out, ffn2_post_g):
    b, s, d = x.shape
    n = b * s
    depth = ffn1_w_in.shape[0]
    cos_t, sin_t = _rope_tables(positions)
    nsa_consts = _nsa_constants(s)
    half_w = (CMP_BLOCK // 2) * NSA_KV * HEAD_DIM
    x = x.reshape(n, d)
    for l in range(depth):
        x = _ffn(x, ffn1_pre_g[l], ffn1_w_in[l].astype(BF16), ffn1_w_out[l].astype(BF16), ffn1_post_g[l])

        uc, up, qn, kc, vc, ksw, gn, qs, kvs, gt = _inproj(x, mix_pre_g[l], cos_t, sin_t, _inproj_weights(w_in[l]))
        ya, yb = _convpool(uc.reshape(b, s, -1), up.reshape(b, s, -1), conv_w[l], conv_b[l],
                           conv_ln_g[l], conv_ln_b[l], pool_w[l], pool_scale[l])
        kcm, vcm = _compress(
            kc.reshape(b, s // CMP_STRIDE, half_w), vc.reshape(b, s // CMP_STRIDE, half_w),
            _compress_weights(cmp_k_pos[l], cmp_k_w1[l], cmp_k_b1[l], cmp_k_w2[l], cmp_k_b2[l]),
            _compress_weights(cmp_v_pos[l], cmp_v_w1[l], cmp_v_b1[l], cmp_v_w2[l], cmp_v_b2[l]))
        yc = _nsa(qn.reshape(b, s, -1), kcm, vcm, ksw.reshape(b, s, -1), gn.reshape(b, s, -1), nsa_consts)
        yd = _swa(qs.reshape(b, s, -1), kvs.reshape(b, s, -1), swa_sinks[l])
        wb = jnp.stack([w_branch[l, 0], w_branch[l, 1], _pair_rows(w_branch[l, 2]), _pair_rows(w_branch[l, 3])])
        x = _merge(x, ya.reshape(n, -1), yb.reshape(n, -1), yc.reshape(n, -1), yd.reshape(n, -1), gt,
                   wb.astype(BF16), w_out[l].astype(BF16), mix_post_g[l])

        kv = _memkv(mem, mem_g[l], w_xkv[l].astype(BF16))
        x = _xattn(x.reshape(b, s, d), x_pre_g[l], w_xq[l].astype(BF16), kv, w_xo[l].astype(BF16),
                   x_post_g[l]).reshape(n, d)

        x = _ffn(x, ffn2_pre_g[l], ffn2_w_in[l].astype(BF16), ffn2_w_out[l].astype(BF16), ffn2_post_g[l])
    return x.reshape(b, s, d)
```

```python
import functools

import numpy as np
import jax
import jax.numpy as jnp
from jax import lax
from jax.experimental import pallas as pl
from jax.experimental.pallas import tpu as pltpu

F32 = jnp.float32
BF16 = jnp.bfloat16

D_MODEL = 1024
DEPTH = 4
HEAD_DIM = 64
ROPE_THETA = 10000.0
NORM_EPS = 1e-6
NEG_INF = -1e30
D_FF = 2816
CONV_W = 512
CONV_K = 31
POOL_W = 512
POOL_GROUPS = 4
POOL_WINDOWS = (2, 4, 8, 16)
NSA_HEADS = 8
NSA_KV = 2
CMP_BLOCK = 32
CMP_STRIDE = 16
CMP_HIDDEN = 256
SLC_BLOCK = 64
N_SEL = 16
FORCE_BONUS = 1e4
NSA_WINDOW = 512
SWA_HEADS = 8
SWA_KV = 2
SWA_WINDOW = 128
X_HEADS = 4
X_HEAD_DIM = D_MODEL // X_HEADS
N_BRANCH = 4
BRANCH_W = 512
IN_SIZES = (2 * CONV_W, POOL_W, NSA_HEADS * HEAD_DIM, 6 * NSA_KV * HEAD_DIM, 3 * NSA_HEADS,
            SWA_HEADS * HEAD_DIM, 2 * SWA_KV * HEAD_DIM, N_BRANCH * D_MODEL)
IN_OFFS = tuple(int(o) for o in np.cumsum((0,) + IN_SIZES))

LANES = 128
VMEM_LIMIT = 56 * 1024 * 1024
HEADS_PER_GROUP = NSA_HEADS // NSA_KV
NSA_Q_TILE = 256
SWA_Q_TILE = 128
SWA_STEP = 512
LOG2E = float(np.log2(np.e))
CONV_T = 128
CONV_HALO = 32
POOL_HALO = 16


def _params(*sem):
    return pltpu.CompilerParams(dimension_semantics=sem, vmem_limit_bytes=VMEM_LIMIT)


def _const_spec(shape):
    nd = len(shape)
    return pl.BlockSpec(shape, lambda *_: (0,) * nd, pipeline_mode=pl.Buffered(1))


def _rms(x, g):
    return x * lax.rsqrt(jnp.mean(x * x, -1, keepdims=True) + NORM_EPS) * g


def _sigmoid(x):
    return 0.5 * jnp.tanh(0.5 * x) + 0.5


def _dot(a, b):
    return jnp.dot(a, b, preferred_element_type=F32)


def _dot_nt(a, b):
    return lax.dot_general(a, b, (((1,), (1,)), ((), ())), preferred_element_type=F32)


def _split_dot(x, e):
    hi = x.astype(BF16)
    lo = (x - hi.astype(F32)).astype(BF16)
    return _dot(hi, e) + _dot(lo, e)


def _rope_kernel(pos_ref, inv_ref, sgn_ref, cos_ref, sin_ref):
    ang = pos_ref[...].astype(F32) * inv_ref[...]
    cos_ref[...] = jnp.cos(ang)
    sin_ref[...] = jnp.sin(ang) * sgn_ref[...]


def _rope_tables(positions):
    n = positions.size
    tm = 2048
    inv = ROPE_THETA ** (-jnp.arange(0, HEAD_DIM, 2, dtype=F32) / HEAD_DIM)
    inv = jnp.tile(inv, LANES // (HEAD_DIM // 2))[None, :]
    sgn = jnp.tile(jnp.concatenate([-jnp.ones(HEAD_DIM // 2, F32), jnp.ones(HEAD_DIM // 2, F32)]),
                   LANES // HEAD_DIM)[None, :]
    return pl.pallas_call(
        _rope_kernel,
        grid=(n // tm,),
        in_specs=[pl.BlockSpec((tm, 1), lambda i: (i, 0)),
                  _const_spec((1, LANES)), _const_spec((1, LANES))],
        out_specs=[pl.BlockSpec((tm, LANES), lambda i: (i, 0))] * 2,
        out_shape=[jax.ShapeDtypeStruct((n, LANES), F32)] * 2,
        compiler_params=_params("parallel"),
        name="rope_tables",
    )(positions.reshape(n, 1), inv, sgn)


def _ffn_kernel(x_ref, pre_ref, wa_ref, wb_ref, wo_ref, post_ref, o_ref):
    x = x_ref[...]
    h = _rms(x, pre_ref[...]).astype(BF16)
    a = _dot(h, wa_ref[...])
    b = _dot(h, wb_ref[...])
    t = (a * _sigmoid(a) * b).astype(BF16)
    y = _dot(t, wo_ref[...])
    o_ref[...] = x + 0.5 * _rms(y, post_ref[...])


def _ffn(x, pre_g, w_in, w_out, post_g, tm=512):
    n, d = x.shape
    f = w_out.shape[0]
    return pl.pallas_call(
        _ffn_kernel,
        grid=(n // tm,),
        in_specs=[pl.BlockSpec((tm, d), lambda i: (i, 0)),
                  _const_spec((1, d)),
                  pl.BlockSpec((d, f), lambda i: (0, 0), pipeline_mode=pl.Buffered(1)),
                  pl.BlockSpec((d, f), lambda i: (0, 1), pipeline_mode=pl.Buffered(1)),
                  _const_spec((f, d)),
                  _const_spec((1, d))],
        out_specs=pl.BlockSpec((tm, d), lambda i: (i, 0)),
        out_shape=jax.ShapeDtypeStruct((n, d), F32),
        compiler_params=_params("parallel"),
        name="ffn",
    )(x, pre_g[None, :], w_in, w_in, w_out, post_g[None, :])


def _inproj_kernel(x_ref, g_ref, cos_ref, sin_ref,
                   w_uc, w_up, w_qn, w_kn, w_gn, w_qs, w_ks, w_gt,
                   uc_ref, up_ref, qn_ref, kc_ref, vc_ref, ksw_ref, gn_ref, qs_ref, kvs_ref, gt_ref):
    h = _rms(x_ref[...], g_ref[...]).astype(BF16)
    cos_t = cos_ref[...]
    sin_t = sin_ref[...]
    lane = lax.broadcasted_iota(jnp.int32, (1, LANES), 1)
    first_half = (lane & (HEAD_DIM // 2)) == 0

    def rope(seg):
        rot = jnp.where(first_half, pltpu.roll(seg, LANES - HEAD_DIM // 2, 1),
                        pltpu.roll(seg, HEAD_DIM // 2, 1))
        return seg * cos_t + rot * sin_t

    def seg(v, j):
        return v[:, j * LANES:(j + 1) * LANES]

    q_scale = HEAD_DIM ** -0.5 * LOG2E
    uc_ref[...] = _dot(h, w_uc[...]).astype(BF16)
    up_ref[...] = _dot(h, w_up[...]).astype(BF16)
    qn = _dot(h, w_qn[...])
    for p in range(NSA_HEADS):
        qn_ref[:, p * LANES:(p + 1) * LANES] = (rope(seg(qn, p)) * q_scale).astype(BF16)
    kn = _dot(h, w_kn[...])
    kc_ref[...] = rope(seg(kn, 0))
    vc_ref[...] = seg(kn, 1)
    ksw_ref[:, 0 * LANES:1 * LANES] = rope(seg(kn, 2)).astype(BF16)
    ksw_ref[:, 1 * LANES:2 * LANES] = seg(kn, 3).astype(BF16)
    ksw_ref[:, 2 * LANES:3 * LANES] = rope(seg(kn, 4)).astype(BF16)
    ksw_ref[:, 3 * LANES:4 * LANES] = seg(kn, 5).astype(BF16)
    gn_ref[...] = _dot(h, w_gn[...])
    qs = _dot(h, w_qs[...])
    for p in range(SWA_HEADS):
        qs_ref[:, p * LANES:(p + 1) * LANES] = (rope(seg(qs, p)) * q_scale).astype(BF16)
    ks = _dot(h, w_ks[...])
    kvs_ref[:, 0:LANES] = rope(seg(ks, 0)).astype(BF16)
    kvs_ref[:, LANES:2 * LANES] = seg(ks, 1).astype(BF16)
    gt_ref[...] = _dot(h, w_gt[...]).astype(BF16)


def _expand_q(wq, n_groups, per_group):
    d = wq.shape[0]
    w = wq.reshape(d, n_groups, per_group, HEAD_DIM)
    w = jnp.einsum('kgrd,gs->kgrsd', w, jnp.eye(n_groups, dtype=wq.dtype))
    return w.reshape(d, n_groups * per_group * n_groups * HEAD_DIM)


def _inproj_weights(w_in):
    o = IN_OFFS
    w_uc = w_in[:, o[0]:o[1]]
    w_up = w_in[:, o[1]:o[2]]
    w_qn = _expand_q(w_in[:, o[2]:o[3]], NSA_KV, HEADS_PER_GROUP)
    w_kn = w_in[:, o[3]:o[4]]
    w_gn = jnp.pad(w_in[:, o[4]:o[5]], ((0, 0), (0, LANES - 3 * NSA_HEADS)))
    w_qs = _expand_q(w_in[:, o[5]:o[6]], SWA_KV, SWA_HEADS // SWA_KV)
    w_ks = w_in[:, o[6]:o[7]]
    w_gt = w_in[:, o[7]:o[8]]
    return tuple(w.astype(BF16) for w in (w_uc, w_up, w_qn, w_kn, w_gn, w_qs, w_ks, w_gt))


def _inproj(x, g, cos_t, sin_t, weights, tm=256):
    n, d = x.shape
    widths = [(2 * CONV_W, BF16), (POOL_W, BF16), (NSA_HEADS * LANES, BF16), (LANES, F32), (LANES, F32),
              (4 * LANES, BF16), (LANES, F32), (SWA_HEADS * LANES, BF16), (2 * LANES, BF16),
              (N_BRANCH * D_MODEL, BF16)]
    return pl.pallas_call(
        _inproj_kernel,
        grid=(n // tm,),
        in_specs=[pl.BlockSpec((tm, d), lambda i: (i, 0)), _const_spec((1, d)),
                  pl.BlockSpec((tm, LANES), lambda i: (i, 0)), pl.BlockSpec((tm, LANES), lambda i: (i, 0))]
                 + [_const_spec(w.shape) for w in weights],
        out_specs=[pl.BlockSpec((tm, w), lambda i: (i, 0)) for w, _ in widths],
        out_shape=[jax.ShapeDtypeStruct((n, w), dt) for w, dt in widths],
        compiler_params=_params("parallel"),
        name="inproj",
    )(x, g[None, :], cos_t, sin_t, *weights)


def _convpool_kernel(uc_ref, up_ref, cw_ref, cb_ref, lng_ref, lnb_ref, pw_ref, psc_ref,
                     ya_ref, yb_ref, vbuf, pbuf, cbuf, *, seq):
    t_c = CONV_T
    vbuf[0:CONV_HALO, :] = jnp.zeros((CONV_HALO, CONV_W), F32)
    pbuf[0:POOL_HALO, :] = jnp.zeros((POOL_HALO, POOL_W), F32)

    def fill(c, carry):
        r0 = pl.multiple_of(c * t_c, t_c)
        u = uc_ref[0, pl.ds(r0, t_c), :].astype(F32)
        vbuf[pl.ds(CONV_HALO + r0, t_c), :] = u[:, :CONV_W] * _sigmoid(u[:, CONV_W:])
        pbuf[pl.ds(POOL_HALO + r0, t_c), :] = up_ref[0, pl.ds(r0, t_c), :].astype(F32)
        return carry

    lax.fori_loop(0, seq // t_c, fill, 0)

    cg = POOL_W // POOL_GROUPS

    def mix(c, carry):
        r0 = pl.multiple_of(c * t_c, t_c)
        for lt in range(CONV_W // LANES):
            ls = slice(lt * LANES, (lt + 1) * LANES)
            xext = vbuf[pl.ds(r0, t_c + CONV_HALO), ls]
            acc = jnp.broadcast_to(cb_ref[:, ls], (t_c, LANES))
            for r in range(8):
                rolled = xext if r == 0 else pltpu.roll(xext, r, 0)
                for a in range(CONV_HALO // 8):
                    shift = 8 * a + r
                    if shift > CONV_K - 1:
                        continue
                    k = CONV_K - 1 - shift
                    base = CONV_HALO - 8 * a
                    acc = acc + rolled[base:base + t_c, :] * cw_ref[k:k + 1, ls]
            cbuf[:, ls] = acc
        y = cbuf[...]
        mu = jnp.mean(y, -1, keepdims=True)
        yc = y - mu
        var = jnp.mean(yc * yc, -1, keepdims=True)
        z = yc * lax.rsqrt(var + NORM_EPS) * lng_ref[...] + lnb_ref[...]
        ya_ref[0, pl.ds(r0, t_c), :] = (z * _sigmoid(z)).astype(BF16)

        pext = pbuf[pl.ds(r0, t_c + POOL_HALO), :]
        e2 = pext + pltpu.roll(pext, 1, 0)
        e4 = e2[:, cg:] + pltpu.roll(e2[:, cg:], 2, 0)
        e8 = e4[:, cg:] + pltpu.roll(e4[:, cg:], 4, 0)
        e16 = e8[:, cg:] + pltpu.roll(e8[:, cg:], 8, 0)
        tpos = r0 + lax.broadcasted_iota(jnp.int32, (t_c, 1), 0)
        for g, (w, e) in enumerate(zip(POOL_WINDOWS, (e2, e4, e8, e16))):
            cnt = jnp.minimum(tpos + 1, w).astype(F32)
            v_g = pext[POOL_HALO:, g * cg:(g + 1) * cg]
            dlt = e[POOL_HALO:, :cg] / cnt - v_g
            yg = _dot(dlt.astype(BF16), pw_ref[g]) * psc_ref[:, g * cg:(g + 1) * cg]
            yb_ref[0, pl.ds(r0, t_c), g * cg:(g + 1) * cg] = yg.astype(BF16)
        return carry

    lax.fori_loop(0, seq // t_c, mix, 0)


def _convpool(uc, up, conv_w, conv_b, ln_g, ln_b, pool_w, pool_scale):
    b, s, _ = uc.shape
    return pl.pallas_call(
        functools.partial(_convpool_kernel, seq=s),
        grid=(b,),
        in_specs=[pl.BlockSpec((1, s, 2 * CONV_W), lambda i: (i, 0, 0)),
                  pl.BlockSpec((1, s, POOL_W), lambda i: (i, 0, 0)),
                  _const_spec((CONV_K, CONV_W)), _const_spec((1, CONV_W)),
                  _const_spec((1, CONV_W)), _const_spec((1, CONV_W)),
                  _const_spec(pool_w.shape), _const_spec((1, POOL_W))],
        out_specs=[pl.BlockSpec((1, s, CONV_W), lambda i: (i, 0, 0)),
                   pl.BlockSpec((1, s, POOL_W), lambda i: (i, 0, 0))],
        out_shape=[jax.ShapeDtypeStruct((b, s, CONV_W), BF16), jax.ShapeDtypeStruct((b, s, POOL_W), BF16)],
        scratch_shapes=[pltpu.VMEM((CONV_HALO + s, CONV_W), F32), pltpu.VMEM((POOL_HALO + s, POOL_W), F32),
                        pltpu.VMEM((CONV_T, CONV_W), F32)],
        compiler_params=_params("parallel"),
        name="convpool",
    )(uc, up, conv_w, conv_b[None, :], ln_g[None, :], ln_b[None, :], pool_w.astype(BF16), pool_scale[None, :])


def _gelu_tanh(x):
    c = np.float32(np.sqrt(2.0 / np.pi))
    return x * (0.5 * (1.0 + jnp.tanh(c * (x + 0.044715 * (x * x * x)))))


def _compress_kernel(kc_ref, vc_ref, pos_ref, w1_ref, b1_ref, w2_ref, b2_ref, ko_ref, vo_ref):
    n_half = kc_ref.shape[1]
    for t, (src, dst) in enumerate(((kc_ref, ko_ref), (vc_ref, vo_ref))):
        x = src[0]
        top = _dot((x + pos_ref[t, 0:1, :]).astype(BF16), w1_ref[t, 0])
        bot = _dot((x + pos_ref[t, 1:2, :]).astype(BF16), w1_ref[t, 1])
        h1 = top + pltpu.roll(bot, n_half - 1, 0) + b1_ref[t]
        out = _dot(_gelu_tanh(h1).astype(BF16), w2_ref[t]) + b2_ref[t]
        dst[0] = out.astype(BF16)


def _compress_weights(pos, w1, b1, w2, b2):
    half = CMP_BLOCK // 2
    eye = jnp.eye(NSA_KV, dtype=F32)
    pos_e = jnp.tile(pos.reshape(2, half, 1, HEAD_DIM), (1, 1, NSA_KV, 1)).reshape(2, half * NSA_KV * HEAD_DIM)
    w1r = w1.reshape(2, half, HEAD_DIM, CMP_HIDDEN)
    w1e = jnp.einsum('tldn,gh->tlgdhn', w1r, eye).reshape(2, half * NSA_KV * HEAD_DIM, NSA_KV * CMP_HIDDEN)
    b1e = jnp.tile(b1, NSA_KV)[None, :]
    w2e = jnp.einsum('nd,gh->gnhd', w2, eye).reshape(NSA_KV * CMP_HIDDEN, NSA_KV * HEAD_DIM)
    b2e = jnp.tile(b2, NSA_KV)[None, :]
    return pos_e, w1e.astype(BF16), b1e, w2e.astype(BF16), b2e


def _compress(kc, vc, kparams, vparams):
    b, nh, width = kc.shape
    stk = [jnp.stack([kp, vp]) for kp, vp in zip(kparams, vparams)]
    return pl.pallas_call(
        _compress_kernel,
        grid=(b,),
        in_specs=[pl.BlockSpec((1, nh, width), lambda i: (i, 0, 0))] * 2 + [_const_spec(a.shape) for a in stk],
        out_specs=[pl.BlockSpec((1, nh, LANES), lambda i: (i, 0, 0))] * 2,
        out_shape=[jax.ShapeDtypeStruct((b, nh, LANES), BF16)] * 2,
        compiler_params=_params("parallel"),
        name="nsa_compress",
    )(kc, vc, *stk)


def _stack_heads(q_all, g):
    return jnp.concatenate(
        [q_all[:, (HEADS_PER_GROUP * g + r) * LANES:(HEADS_PER_GROUP * g + r + 1) * LANES]
         for r in range(HEADS_PER_GROUP)], axis=0)


def _pair_groups(o_g0, o_g1, r, tq):
    lane = lax.broadcasted_iota(jnp.int32, (1, LANES), 1)
    return jnp.where(lane < HEAD_DIM, o_g0[r * tq:(r + 1) * tq], o_g1[r * tq:(r + 1) * tq])


def _with_ones(v, g):
    lane = lax.broadcasted_iota(jnp.int32, (1, LANES), 1)
    own = (lane < HEAD_DIM) if g == 0 else (lane >= HEAD_DIM)
    return jnp.where(own, v, jnp.ones_like(v))


def _pair_normalized(acc_g0, acc_g1, r, tq, extra=None):
    lane = lax.broadcasted_iota(jnp.int32, (1, LANES), 1)
    rows = slice(r * tq, (r + 1) * tq)
    a0, a1 = acc_g0[rows], acc_g1[rows]
    num = jnp.where(lane < HEAD_DIM, a0, a1)
    den = pltpu.roll(jnp.where(lane < HEAD_DIM, a1, a0), HEAD_DIM, 1)
    if extra is not None:
        den = jnp.where(lane < HEAD_DIM, den + extra[0][rows], den + extra[1][rows])
    return num / den


def _band_bias(tq, window, n_variants):
    q = np.arange(tq)[:, None]
    k = np.arange(window + tq)[None, :]
    out = np.full((n_variants, tq, window + tq), NEG_INF, np.float32)
    for v in range(n_variants):
        diff = min(v * tq, window) + q - k
        out[v][(diff >= 0) & (diff < window)] = 0.0
    return jnp.asarray(out)


def _add_tiled(s, bias, tq):
    return jnp.concatenate([s[r * tq:(r + 1) * tq] + bias for r in range(HEADS_PER_GROUP)], axis=0)


def _pair_rows(w):
    d = w.shape[1]
    return w.reshape(NSA_KV, HEADS_PER_GROUP, HEAD_DIM, d).transpose(1, 0, 2, 3).reshape(-1, d)


def _nsa_kernel(q_ref, kc_ref, vc_ref, ks_ref, vs_ref, kw_ref, vw_ref, gl_ref, ovt_ref, esel_ref, eg_ref,
                wbias_ref, o_ref, *, seq):
    tq = NSA_Q_TILE
    m_rows = HEADS_PER_GROUP * tq
    n_cmp = seq // CMP_STRIDE
    n_sel = seq // SLC_BLOCK
    k_top = min(N_SEL, n_sel)
    w_keys = NSA_WINDOW + tq
    qi = pl.program_id(1)
    q0 = pl.multiple_of(qi * tq, tq)
    q_all = q_ref[0]
    row = lax.broadcasted_iota(jnp.int32, (m_rows, 1), 0)
    qpos = q0 + (row & (tq - 1))
    kcm = kc_ref[0]
    vcm = vc_ref[0]
    cidx = lax.broadcasted_iota(jnp.int32, (1, n_cmp), 1)
    cmp_valid = ((cidx * CMP_STRIDE + (CMP_BLOCK - 1)) <= qpos) & (cidx < n_cmp - 1)
    jidx = lax.broadcasted_iota(jnp.int32, (n_sel, 1), 0)
    qpos_t = q0 + lax.broadcasted_iota(jnp.int32, (1, tq), 1)
    forced = (jidx == 0) | (jidx == jnp.right_shift(qpos_t, int(np.log2(SLC_BLOCK))))
    causal_blk = (jidx * SLC_BLOCK) <= qpos_t
    kstart = pl.multiple_of(jnp.maximum(q0 - NSA_WINDOW, 0), tq)
    n_var = wbias_ref.shape[0]
    win_bias = wbias_ref[jnp.minimum(qi, n_var - 1)]
    diag_bias = wbias_ref[n_var - 1, :, NSA_WINDOW:]

    q_plain, q_aug, o_cmp = [], [], []
    for g in range(NSA_KV):
        qg = _stack_heads(q_all, g)
        q_plain.append(qg)

        s_c = _dot_nt(qg, kcm)
        sm = jnp.where(cmp_valid, s_c, NEG_INF)
        e = jnp.where(cmp_valid, jnp.exp2(sm - jnp.max(sm, -1, keepdims=True)), 0.0)
        p_c = e / jnp.maximum(jnp.sum(e, -1, keepdims=True), 1e-30)
        o_cmp.append(_dot(p_c.astype(BF16), vcm))

        psum = p_c[0:tq]
        for r in range(1, HEADS_PER_GROUP):
            psum = psum + p_c[r * tq:(r + 1) * tq]
        p_hi = psum.astype(BF16)
        p_lo = (psum - p_hi.astype(F32)).astype(BF16)
        ovt = ovt_ref[...]
        imp_t = (_dot_nt(ovt, p_hi) + _dot_nt(ovt, p_lo))[0:n_sel]
        score = jnp.where(causal_blk, imp_t + jnp.where(forced, FORCE_BONUS, 0.0), NEG_INF)
        rank = jnp.zeros((n_sel, tq), jnp.int32)
        for i in range(n_sel):
            ri = score[i:i + 1, :]
            rank = rank + jnp.where(jidx > i, jnp.where(ri >= score, 1, 0), jnp.where(ri > score, 1, 0))
        bias_t = jnp.where(rank < k_top, 0.0, NEG_INF)
        bias_t = jnp.concatenate([bias_t, jnp.zeros((LANES - n_sel, tq), F32)], axis=0)
        bias = bias_t.T.astype(BF16)
        q_aug.append(jnp.concatenate([qg, jnp.concatenate([bias] * HEADS_PER_GROUP, axis=0)], axis=1))

    def sel_tile(kt, carry, diagonal):
        k0 = pl.multiple_of(kt * tq, tq)
        k_aug = jnp.concatenate([ks_ref[0, pl.ds(k0, tq), :], esel_ref[pl.ds(k0, tq), :]], axis=1)
        v_t = vs_ref[0, pl.ds(k0, tq), :]
        new = []
        for g in range(NSA_KV):
            m_i, acc = carry[g]
            s = _dot_nt(q_aug[g], k_aug)
            if diagonal:
                s = _add_tiled(s, diag_bias, tq)
            m_n = jnp.maximum(m_i, jnp.max(s, -1, keepdims=True))
            p = jnp.exp2(s - m_n).astype(BF16)
            acc_n = jnp.exp2(m_i - m_n) * acc + _dot(p, _with_ones(v_t, g))
            new.append((m_n, acc_n))
        return tuple(new)

    init = tuple((jnp.full((m_rows, 1), NEG_INF, F32), jnp.zeros((m_rows, LANES), F32))
                 for _ in range(NSA_KV))
    carry = lax.fori_loop(0, qi, functools.partial(sel_tile, diagonal=False), init)
    carry = sel_tile(qi, carry, True)
    o_sel = [carry[g][1] for g in range(NSA_KV)]

    o_win = []
    k_w = kw_ref[0, pl.ds(kstart, w_keys), :]
    v_w = vw_ref[0, pl.ds(kstart, w_keys), :]
    for g in range(NSA_KV):
        s_w = _add_tiled(_dot_nt(q_plain[g], k_w), win_bias, tq)
        p_w = jnp.exp2(s_w - jnp.max(s_w, -1, keepdims=True)).astype(BF16)
        o_win.append(_dot(p_w, _with_ones(v_w, g)))

    gates = _sigmoid(gl_ref[0])
    g_exp = [_split_dot(gates, eg_ref[j]) for j in range(3)]
    for r in range(HEADS_PER_GROUP):
        ls = slice(r * LANES, (r + 1) * LANES)
        y = (g_exp[0][:, ls] * _pair_groups(o_cmp[0], o_cmp[1], r, tq)
             + g_exp[1][:, ls] * _pair_normalized(o_sel[0], o_sel[1], r, tq)
             + g_exp[2][:, ls] * _pair_normalized(o_win[0], o_win[1], r, tq))
        o_ref[0, :, ls] = y.astype(BF16)


def _nsa_constants(seq):
    n_cmp = seq // CMP_STRIDE
    n_sel = seq // SLC_BLOCK
    cs = np.arange(n_cmp) * CMP_STRIDE
    ss = np.arange(n_sel) * SLC_BLOCK
    ov = (cs[:, None] <= ss[None, :] + SLC_BLOCK - 1) & (cs[:, None] + CMP_BLOCK - 1 >= ss[None, :])
    ov[n_cmp - 1, :] = False
    ovt = np.zeros((LANES, n_cmp), np.float32)
    ovt[:n_sel, :] = ov.T
    esel = np.zeros((seq, LANES), np.float32)
    esel[np.arange(seq), np.arange(seq) // SLC_BLOCK] = 1.0
    eg = np.zeros((3, LANES, NSA_HEADS * HEAD_DIM), np.float32)
    for g in range(NSA_KV):
        for r in range(HEADS_PER_GROUP):
            for j in range(3):
                c0 = r * LANES + g * HEAD_DIM
                eg[j, (g * HEADS_PER_GROUP + r) * 3 + j, c0:c0 + HEAD_DIM] = 1.0
    wbias = _band_bias(NSA_Q_TILE, NSA_WINDOW, NSA_WINDOW // NSA_Q_TILE + 1)
    return jnp.asarray(ovt, BF16), jnp.asarray(esel, BF16), jnp.asarray(eg, BF16), wbias


def _nsa(qn, kcm, vcm, ksw, gn, consts):
    b, s, _ = qn.shape
    tq = NSA_Q_TILE
    assert s % tq == 0 and s >= NSA_WINDOW + tq and s // SLC_BLOCK <= LANES
    n_cmp = s // CMP_STRIDE
    ovt, esel, eg, wbias = consts
    kv_spec = lambda j: pl.BlockSpec((1, s, LANES), lambda i, t: (i, 0, j))
    return pl.pallas_call(
        functools.partial(_nsa_kernel, seq=s),
        grid=(b, s // tq),
        in_specs=[pl.BlockSpec((1, tq, NSA_HEADS * LANES), lambda i, t: (i, t, 0)),
                  pl.BlockSpec((1, n_cmp, LANES), lambda i, t: (i, 0, 0)),
                  pl.BlockSpec((1, n_cmp, LANES), lambda i, t: (i, 0, 0)),
                  kv_spec(0), kv_spec(1), kv_spec(2), kv_spec(3),
                  pl.BlockSpec((1, tq, LANES), lambda i, t: (i, t, 0)),
                  _const_spec(ovt.shape), _const_spec(esel.shape), _const_spec(eg.shape),
                  _const_spec(wbias.shape)],
        out_specs=pl.BlockSpec((1, tq, NSA_HEADS * HEAD_DIM), lambda i, t: (i, t, 0)),
        out_shape=jax.ShapeDtypeStruct((b, s, NSA_HEADS * HEAD_DIM), BF16),
        compiler_params=_params("parallel", "arbitrary"),
        name="nsa_attention",
    )(qn, kcm, vcm, ksw, ksw, ksw, ksw, gn, ovt, esel, eg, wbias)


def _swa_kernel(sink_ref, q_ref, k_ref, v_ref, bias_ref, o_ref):
    tq = SWA_Q_TILE
    w_keys = SWA_WINDOW + tq
    base = pl.program_id(1) * SWA_STEP
    sinks = [
        jnp.concatenate([jnp.full((tq, LANES), sink_ref[HEADS_PER_GROUP * g + r] * LOG2E, F32)
                         for r in range(HEADS_PER_GROUP)], axis=0)
        for g in range(SWA_KV)]
    for t in range(SWA_STEP // tq):
        q0 = base + t * tq
        q_all = q_ref[0, t * tq:(t + 1) * tq, :]
        kstart = pl.multiple_of(jnp.maximum(q0 - SWA_WINDOW, 0), tq)
        bias = bias_ref[1] if t > 0 else bias_ref[jnp.minimum(pl.program_id(1), 1)]
        k_t = k_ref[0, pl.ds(kstart, w_keys), :]
        v_t = v_ref[0, pl.ds(kstart, w_keys), :]
        accs, sink_terms = [], []
        for g in range(SWA_KV):
            s = _add_tiled(_dot_nt(_stack_heads(q_all, g), k_t), bias, tq)
            m = jnp.maximum(jnp.max(s, -1, keepdims=True), sinks[g])
            p = jnp.exp2(s - jnp.concatenate([m] * (w_keys // LANES), axis=1)).astype(BF16)
            accs.append(_dot(p, _with_ones(v_t, g)))
            sink_terms.append(jnp.exp2(sinks[g] - m))
        for r in range(HEADS_PER_GROUP):
            o_ref[0, t * tq:(t + 1) * tq, r * LANES:(r + 1) * LANES] = (
                _pair_normalized(accs[0], accs[1], r, tq, sink_terms).astype(BF16))


def _swa(qs, kvs, sinks):
    b, s, _ = qs.shape
    assert s % SWA_STEP == 0 and s >= SWA_WINDOW + SWA_Q_TILE
    bias = _band_bias(SWA_Q_TILE, SWA_WINDOW, SWA_WINDOW // SWA_Q_TILE + 1)
    return pl.pallas_call(
        _swa_kernel,
        grid=(b, s // SWA_STEP),
        in_specs=[pl.BlockSpec(memory_space=pltpu.SMEM),
                  pl.BlockSpec((1, SWA_STEP, SWA_HEADS * LANES), lambda i, t: (i, t, 0)),
                  pl.BlockSpec((1, s, LANES), lambda i, t: (i, 0, 0)),
                  pl.BlockSpec((1, s, LANES), lambda i, t: (i, 0, 1)),
                  _const_spec(bias.shape)],
        out_specs=pl.BlockSpec((1, SWA_STEP, SWA_HEADS * HEAD_DIM), lambda i, t: (i, t, 0)),
        out_shape=jax.ShapeDtypeStruct((b, s, SWA_HEADS * HEAD_DIM), BF16),
        compiler_params=_params("parallel", "arbitrary"),
        name="swa_attention",
    )(sinks, qs, kvs, kvs, bias)


def _merge_kernel(x_ref, ya_ref, yb_ref, yc_ref, yd_ref, gt_ref, wb_ref, wo_ref, g_ref, o_ref):
    merged = None
    for n, y_ref in enumerate((ya_ref, yb_ref, yc_ref, yd_ref)):
        z = _dot(y_ref[...], wb_ref[n])
        gate = _sigmoid(gt_ref[:, n * D_MODEL:(n + 1) * D_MODEL].astype(F32))
        merged = gate * z if merged is None else merged + gate * z
    out = _dot(merged.astype(BF16), wo_ref[...])
    o_ref[...] = x_ref[...] + _rms(out, g_ref[...])


def _merge(x, ya, yb, yc, yd, gt, wb, wo, g, tm=512):
    n, d = x.shape
    row = lambda w: pl.BlockSpec((tm, w), lambda i: (i, 0))
    return pl.pallas_call(
        _merge_kernel,
        grid=(n // tm,),
        in_specs=[row(d), row(BRANCH_W), row(BRANCH_W), row(BRANCH_W), row(BRANCH_W), row(N_BRANCH * d),
                  _const_spec(wb.shape), _const_spec(wo.shape), _const_spec((1, d))],
        out_specs=row(d),
        out_shape=jax.ShapeDtypeStruct((n, d), F32),
        compiler_params=_params("parallel"),
        name="merge",
    )(x, ya, yb, yc, yd, gt, wb, wo, g[None, :])


def _memkv_kernel(mem_ref, g_ref, w_ref, o_ref):
    mn = _rms(mem_ref[0], g_ref[...]).astype(BF16)
    o_ref[0] = _dot(mn, w_ref[...]).astype(BF16)


def _memkv(mem, g, w_kv):
    b, m, d = mem.shape
    return pl.pallas_call(
        _memkv_kernel,
        grid=(b,),
        in_specs=[pl.BlockSpec((1, m, d), lambda i: (i, 0, 0)), _const_spec((1, d)), _const_spec(w_kv.shape)],
        out_specs=pl.BlockSpec((1, m, 2 * d), lambda i: (i, 0, 0)),
        out_shape=jax.ShapeDtypeStruct((b, m, 2 * d), BF16),
        compiler_params=_params("parallel"),
        name="mem_kv",
    )(mem, g[None, :], w_kv)


def _xattn_kernel(x_ref, pre_ref, wq_ref, kv_ref, wo_ref, post_ref, o_ref):
    x = x_ref[0]
    h = _rms(x, pre_ref[...]).astype(BF16)
    q = (_dot(h, wq_ref[...]) * (X_HEAD_DIM ** -0.5)).astype(BF16)
    heads = []
    for hd in range(X_HEADS):
        k_h = kv_ref[0, :, hd * X_HEAD_DIM:(hd + 1) * X_HEAD_DIM]
        v_h = kv_ref[0, :, D_MODEL + hd * X_HEAD_DIM:D_MODEL + (hd + 1) * X_HEAD_DIM]
        s = _dot_nt(q[:, hd * X_HEAD_DIM:(hd + 1) * X_HEAD_DIM], k_h)
        e = jnp.exp(s - jnp.max(s, -1, keepdims=True))
        p = e / jnp.sum(e, -1, keepdims=True)
        heads.append(_dot(p.astype(BF16), v_h).astype(BF16))
    o = jnp.concatenate(heads, axis=1)
    y = _dot(o, wo_ref[...])
    o_ref[0] = x + _rms(y, post_ref[...])


def _xattn(x, pre_g, w_q, kv, w_o, post_g, tm=512):
    b, s, d = x.shape
    m = kv.shape[1]
    return pl.pallas_call(
        _xattn_kernel,
        grid=(b, s // tm),
        in_specs=[pl.BlockSpec((1, tm, d), lambda i, t: (i, t, 0)), _const_spec((1, d)), _const_spec(w_q.shape),
                  pl.BlockSpec((1, m, 2 * d), lambda i, t: (i, 0, 0)), _const_spec(w_o.shape), _const_spec((1, d))],
        out_specs=pl.BlockSpec((1, tm, d), lambda i, t: (i, t, 0)),
        out_shape=jax.ShapeDtypeStruct((b, s, d), F32),
        compiler_params=_params("parallel", "arbitrary"),
        name="cross_attention",
    )(x, pre_g[None, :], w_q, kv, w_o, post_g[None, :])


def kernel(x, mem, positions, ffn1_pre_g, ffn1_w_in, ffn1_w_out, ffn1_post_g, mix_pre_g, w_in, conv_w, conv_b, conv_ln_g, conv_ln_b, pool_w, pool_scale, cmp_k_pos, cmp_k_w1, cmp_k_b1, cmp_k_w2, cmp_k_b2, cmp_v_pos, cmp_v_w1, cmp_v_b1, cmp_v_w2, cmp_v_b2, swa_sinks, w_branch, w_out, mix_post_g, x_pre_g, mem_g, w_xq, w_xkv, w_xo, x_post_g, ffn2_pre_g, ffn2_w_in, ffn2_w_out, ffn2_post_g):
    b, s, d = x.shape
    n = b * s
    depth = ffn1_w_in.shape[0]
    cos_t, sin_t = _rope_tables(positions)
    nsa_consts = _nsa_constants(s)
    half_w = (CMP_BLOCK // 2) * NSA_KV * HEAD_DIM
    x = x.reshape(n, d)
    for l in range(depth):
        x = _ffn(x, ffn1_pre_g[l], ffn1_w_in[l].astype(BF16), ffn1_w_out[l].astype(BF16), ffn1_post_g[l])

        uc, up, qn, kc, vc, ksw, gn, qs, kvs, gt = _inproj(x, mix_pre_g[l], cos_t, sin_t, _inproj_weights(w_in[l]))
        ya, yb = _convpool(uc.reshape(b, s, -1), up.reshape(b, s, -1), conv_w[l], conv_b[l],
                           conv_ln_g[l], conv_ln_b[l], pool_w[l], pool_scale[l])
        kcm, vcm = _compress(
            kc.reshape(b, s // CMP_STRIDE, half_w), vc.reshape(b, s // CMP_STRIDE, half_w),
            _compress_weights(cmp_k_pos[l], cmp_k_w1[l], cmp_k_b1[l], cmp_k_w2[l], cmp_k_b2[l]),
            _compress_weights(cmp_v_pos[l], cmp_v_w1[l], cmp_v_b1[l], cmp_v_w2[l], cmp_v_b2[l]))
        yc = _nsa(qn.reshape(b, s, -1), kcm, vcm, ksw.reshape(b, s, -1), gn.reshape(b, s, -1), nsa_consts)
        yd = _swa(qs.reshape(b, s, -1), kvs.reshape(b, s, -1), swa_sinks[l])
        wb = jnp.stack([w_branch[l, 0], w_branch[l, 1], _pair_rows(w_branch[l, 2]), _pair_rows(w_branch[l, 3])])
        x = _merge(x, ya.reshape(n, -1), yb.reshape(n, -1), yc.reshape(n, -1), yd.reshape(n, -1), gt,
                   wb.astype(BF16), w_out[l].astype(BF16), mix_post_g[l])

        kv = _memkv(mem, mem_g[l], w_xkv[l].astype(BF16))
        x = _xattn(x.reshape(b, s, d), x_pre_g[l], w_xq[l].astype(BF16), kv, w_xo[l].astype(BF16),
                   x_post_g[l]).reshape(n, d)

        x = _ffn(x, ffn2_pre_g[l], ffn2_w_in[l].astype(BF16), ffn2_w_out[l].astype(BF16), ffn2_post_g[l])
    return x.reshape(b, s, d)
```

```python
import functools

import numpy as np
import jax
import jax.numpy as jnp
from jax import lax
from jax.experimental import pallas as pl
from jax.experimental.pallas import tpu as pltpu

F32 = jnp.float32
BF16 = jnp.bfloat16

D_MODEL = 1024
DEPTH = 4
HEAD_DIM = 64
ROPE_THETA = 10000.0
NORM_EPS = 1e-6
NEG_INF = -1e30
D_FF = 2816
CONV_W = 512
CONV_K = 31
POOL_W = 512
POOL_GROUPS = 4
POOL_WINDOWS = (2, 4, 8, 16)
NSA_HEADS = 8
NSA_KV = 2
CMP_BLOCK = 32
CMP_STRIDE = 16
CMP_HIDDEN = 256
SLC_BLOCK = 64
N_SEL = 16
FORCE_BONUS = 1e4
NSA_WINDOW = 512
SWA_HEADS = 8
SWA_KV = 2
SWA_WINDOW = 128
X_HEADS = 4
X_HEAD_DIM = D_MODEL // X_HEADS
N_BRANCH = 4
BRANCH_W = 512
IN_SIZES = (2 * CONV_W, POOL_W, NSA_HEADS * HEAD_DIM, 6 * NSA_KV * HEAD_DIM, 3 * NSA_HEADS,
            SWA_HEADS * HEAD_DIM, 2 * SWA_KV * HEAD_DIM, N_BRANCH * D_MODEL)
IN_OFFS = tuple(int(o) for o in np.cumsum((0,) + IN_SIZES))

LANES = 128
VMEM_LIMIT = 56 * 1024 * 1024
HEADS_PER_GROUP = NSA_HEADS // NSA_KV
NSA_Q_TILE = 256
SWA_Q_TILE = 128
SWA_STEP = 512
LOG2E = float(np.log2(np.e))
CONV_T = 128
CONV_HALO = 32
POOL_HALO = 16


def _params(*sem):
    return pltpu.CompilerParams(dimension_semantics=sem, vmem_limit_bytes=VMEM_LIMIT)


def _const_spec(shape):
    nd = len(shape)
    return pl.BlockSpec(shape, lambda *_: (0,) * nd, pipeline_mode=pl.Buffered(1))


def _rms(x, g):
    return x * lax.rsqrt(jnp.mean(x * x, -1, keepdims=True) + NORM_EPS) * g


def _sigmoid(x):
    return 0.5 * jnp.tanh(0.5 * x) + 0.5


def _dot(a, b):
    return jnp.dot(a, b, preferred_element_type=F32)


def _dot_nt(a, b):
    return lax.dot_general(a, b, (((1,), (1,)), ((), ())), preferred_element_type=F32)


def _split_dot(x, e):
    hi = x.astype(BF16)
    lo = (x - hi.astype(F32)).astype(BF16)
    return _dot(hi, e) + _dot(lo, e)


def _rope_kernel(pos_ref, inv_ref, sgn_ref, cos_ref, sin_ref):
    ang = pos_ref[...].astype(F32) * inv_ref[...]
    cos_ref[...] = jnp.cos(ang)
    sin_ref[...] = jnp.sin(ang) * sgn_ref[...]


def _rope_tables(positions):
    n = positions.size
    tm = 2048
    inv = ROPE_THETA ** (-jnp.arange(0, HEAD_DIM, 2, dtype=F32) / HEAD_DIM)
    inv = jnp.tile(inv, LANES // (HEAD_DIM // 2))[None, :]
    sgn = jnp.tile(jnp.concatenate([-jnp.ones(HEAD_DIM // 2, F32), jnp.ones(HEAD_DIM // 2, F32)]),
                   LANES // HEAD_DIM)[None, :]
    return pl.pallas_call(
        _rope_kernel,
        grid=(n // tm,),
        in_specs=[pl.BlockSpec((tm, 1), lambda i: (i, 0)),
                  _const_spec((1, LANES)), _const_spec((1, LANES))],
        out_specs=[pl.BlockSpec((tm, LANES), lambda i: (i, 0))] * 2,
        out_shape=[jax.ShapeDtypeStruct((n, LANES), F32)] * 2,
        compiler_params=_params("parallel"),
        name="rope_tables",
    )(positions.reshape(n, 1), inv, sgn)


def _ffn_kernel(x_ref, pre_ref, wa_ref, wb_ref, wo_ref, post_ref, o_ref):
    x = x_ref[...]
    h = _rms(x, pre_ref[...]).astype(BF16)
    a = _dot(h, wa_ref[...])
    b = _dot(h, wb_ref[...])
    t = (a * _sigmoid(a) * b).astype(BF16)
    y = _dot(t, wo_ref[...])
    o_ref[...] = x + 0.5 * _rms(y, post_ref[...])


def _ffn(x, pre_g, w_in, w_out, post_g, tm=512):
    n, d = x.shape
    f = w_out.shape[0]
    return pl.pallas_call(
        _ffn_kernel,
        grid=(n // tm,),
        in_specs=[pl.BlockSpec((tm, d), lambda i: (i, 0)),
                  _const_spec((1, d)),
                  pl.BlockSpec((d, f), lambda i: (0, 0), pipeline_mode=pl.Buffered(1)),
                  pl.BlockSpec((d, f), lambda i: (0, 1), pipeline_mode=pl.Buffered(1)),
                  _const_spec((f, d)),
                  _const_spec((1, d))],
        out_specs=pl.BlockSpec((tm, d), lambda i: (i, 0)),
        out_shape=jax.ShapeDtypeStruct((n, d), F32),
        compiler_params=_params("parallel"),
        name="ffn",
    )(x, pre_g[None, :], w_in, w_in, w_out, post_g[None, :])


def _inproj_kernel(x_ref, g_ref, cos_ref, sin_ref,
                   w_uc, w_up, w_qn, w_kn, w_gn, w_qs, w_ks, w_gt,
                   uc_ref, up_ref, qn_ref, kc_ref, vc_ref, ksw_ref, gn_ref, qs_ref, kvs_ref, gt_ref):
    h = _rms(x_ref[...], g_ref[...]).astype(BF16)
    cos_t = cos_ref[...]
    sin_t = sin_ref[...]
    lane = lax.broadcasted_iota(jnp.int32, (1, LANES), 1)
    first_half = (lane & (HEAD_DIM // 2)) == 0

    def rope(seg):
        rot = jnp.where(first_half, pltpu.roll(seg, LANES - HEAD_DIM // 2, 1),
                        pltpu.roll(seg, HEAD_DIM // 2, 1))
        return seg * cos_t + rot * sin_t

    def seg(v, j):
        return v[:, j * LANES:(j + 1) * LANES]

    q_scale = HEAD_DIM ** -0.5 * LOG2E
    low_half = lane < HEAD_DIM

    def place_heads(q, out_ref, n_heads, n_groups):
        per_group = n_heads // n_groups
        for j in range(n_heads // 2):
            pair = rope(seg(q, j)) * q_scale
            swapped = pltpu.roll(pair, HEAD_DIM, 1)
            for k in range(2):
                head = 2 * j + k
                g = head // per_group
                src_tile = pair if k == g else swapped
                keep = low_half if g == 0 else jnp.logical_not(low_half)
                out_ref[head] = jnp.where(keep, src_tile, 0.0).astype(BF16)

    uc_ref[...] = _dot(h, w_uc[...]).astype(BF16)
    up_ref[...] = _dot(h, w_up[...]).astype(BF16)
    place_heads(_dot(h, w_qn[...]), qn_ref, NSA_HEADS, NSA_KV)
    kn = _dot(h, w_kn[...])
    kc_ref[...] = rope(seg(kn, 0))
    vc_ref[...] = seg(kn, 1)
    ksw_ref[:, 0 * LANES:1 * LANES] = rope(seg(kn, 2)).astype(BF16)
    ksw_ref[:, 1 * LANES:2 * LANES] = seg(kn, 3).astype(BF16)
    ksw_ref[:, 2 * LANES:3 * LANES] = rope(seg(kn, 4)).astype(BF16)
    ksw_ref[:, 3 * LANES:4 * LANES] = seg(kn, 5).astype(BF16)
    gn_ref[...] = _dot(h, w_gn[...])
    place_heads(_dot(h, w_qs[...]), qs_ref, SWA_HEADS, SWA_KV)
    ks = _dot(h, w_ks[...])
    kvs_ref[:, 0:LANES] = rope(seg(ks, 0)).astype(BF16)
    kvs_ref[:, LANES:2 * LANES] = seg(ks, 1).astype(BF16)
    gt_ref[...] = _dot(h, w_gt[...]).astype(BF16)


def _inproj_weights(w_in):
    o = IN_OFFS
    w_uc = w_in[:, o[0]:o[1]]
    w_up = w_in[:, o[1]:o[2]]
    w_qn = w_in[:, o[2]:o[3]]
    w_kn = w_in[:, o[3]:o[4]]
    w_gn = jnp.pad(w_in[:, o[4]:o[5]], ((0, 0), (0, LANES - 3 * NSA_HEADS)))
    w_qs = w_in[:, o[5]:o[6]]
    w_ks = w_in[:, o[6]:o[7]]
    w_gt = w_in[:, o[7]:o[8]]
    return tuple(w.astype(BF16) for w in (w_uc, w_up, w_qn, w_kn, w_gn, w_qs, w_ks, w_gt))


def _inproj(x, g, cos_t, sin_t, weights, tm=256):
    n, d = x.shape
    widths = [(2 * CONV_W, BF16, False), (POOL_W, BF16, False), (NSA_HEADS * LANES, BF16, True),
              (LANES, F32, False), (LANES, F32, False), (4 * LANES, BF16, False), (LANES, F32, False),
              (SWA_HEADS * LANES, BF16, True), (2 * LANES, BF16, False), (N_BRANCH * D_MODEL, BF16, False)]
    return pl.pallas_call(
        _inproj_kernel,
        grid=(n // tm,),
        in_specs=[pl.BlockSpec((tm, d), lambda i: (i, 0)), _const_spec((1, d)),
                  pl.BlockSpec((tm, LANES), lambda i: (i, 0)), pl.BlockSpec((tm, LANES), lambda i: (i, 0))]
                 + [_const_spec(w.shape) for w in weights],
        out_specs=[pl.BlockSpec((w // LANES, tm, LANES), lambda i: (0, i, 0)) if hm
                   else pl.BlockSpec((tm, w), lambda i: (i, 0)) for w, _, hm in widths],
        out_shape=[jax.ShapeDtypeStruct((w // LANES, n, LANES) if hm else (n, w), dt) for w, dt, hm in widths],
        compiler_params=_params("parallel"),
        name="inproj",
    )(x, g[None, :], cos_t, sin_t, *weights)


def _convpool_kernel(uc_ref, up_ref, cw_ref, cb_ref, lng_ref, lnb_ref, pw_ref, psc_ref,
                     ya_ref, yb_ref, vbuf, pbuf, cbuf, *, seq):
    t_c = CONV_T
    vbuf[0:CONV_HALO, :] = jnp.zeros((CONV_HALO, CONV_W), F32)
    pbuf[0:POOL_HALO, :] = jnp.zeros((POOL_HALO, POOL_W), F32)

    def fill(c, carry):
        r0 = pl.multiple_of(c * t_c, t_c)
        u = uc_ref[0, pl.ds(r0, t_c), :].astype(F32)
        vbuf[pl.ds(CONV_HALO + r0, t_c), :] = u[:, :CONV_W] * _sigmoid(u[:, CONV_W:])
        pbuf[pl.ds(POOL_HALO + r0, t_c), :] = up_ref[0, pl.ds(r0, t_c), :].astype(F32)
        return carry

    lax.fori_loop(0, seq // t_c, fill, 0)

    cg = POOL_W // POOL_GROUPS

    def mix(c, carry):
        r0 = pl.multiple_of(c * t_c, t_c)
        for lt in range(CONV_W // LANES):
            ls = slice(lt * LANES, (lt + 1) * LANES)
            xext = vbuf[pl.ds(r0, t_c + CONV_HALO), ls]
            acc = jnp.broadcast_to(cb_ref[:, ls], (t_c, LANES))
            for r in range(8):
                rolled = xext if r == 0 else pltpu.roll(xext, r, 0)
                for a in range(CONV_HALO // 8):
                    shift = 8 * a + r
                    if shift > CONV_K - 1:
                        continue
                    k = CONV_K - 1 - shift
                    base = CONV_HALO - 8 * a
                    acc = acc + rolled[base:base + t_c, :] * cw_ref[k:k + 1, ls]
            cbuf[:, ls] = acc
        y = cbuf[...]
        mu = jnp.mean(y, -1, keepdims=True)
        yc = y - mu
        var = jnp.mean(yc * yc, -1, keepdims=True)
        z = yc * lax.rsqrt(var + NORM_EPS) * lng_ref[...] + lnb_ref[...]
        ya_ref[0, pl.ds(r0, t_c), :] = (z * _sigmoid(z)).astype(BF16)

        pext = pbuf[pl.ds(r0, t_c + POOL_HALO), :]
        e2 = pext + pltpu.roll(pext, 1, 0)
        e4 = e2[:, cg:] + pltpu.roll(e2[:, cg:], 2, 0)
        e8 = e4[:, cg:] + pltpu.roll(e4[:, cg:], 4, 0)
        e16 = e8[:, cg:] + pltpu.roll(e8[:, cg:], 8, 0)
        tpos = r0 + lax.broadcasted_iota(jnp.int32, (t_c, 1), 0)
        for g, (w, e) in enumerate(zip(POOL_WINDOWS, (e2, e4, e8, e16))):
            cnt = jnp.minimum(tpos + 1, w).astype(F32)
            v_g = pext[POOL_HALO:, g * cg:(g + 1) * cg]
            dlt = e[POOL_HALO:, :cg] / cnt - v_g
            yg = _dot(dlt.astype(BF16), pw_ref[g]) * psc_ref[:, g * cg:(g + 1) * cg]
            yb_ref[0, pl.ds(r0, t_c), g * cg:(g + 1) * cg] = yg.astype(BF16)
        return carry

    lax.fori_loop(0, seq // t_c, mix, 0)


def _convpool(uc, up, conv_w, conv_b, ln_g, ln_b, pool_w, pool_scale):
    b, s, _ = uc.shape
    return pl.pallas_call(
        functools.partial(_convpool_kernel, seq=s),
        grid=(b,),
        in_specs=[pl.BlockSpec((1, s, 2 * CONV_W), lambda i: (i, 0, 0)),
                  pl.BlockSpec((1, s, POOL_W), lambda i: (i, 0, 0)),
                  _const_spec((CONV_K, CONV_W)), _const_spec((1, CONV_W)),
                  _const_spec((1, CONV_W)), _const_spec((1, CONV_W)),
                  _const_spec(pool_w.shape), _const_spec((1, POOL_W))],
        out_specs=[pl.BlockSpec((1, s, CONV_W), lambda i: (i, 0, 0)),
                   pl.BlockSpec((1, s, POOL_W), lambda i: (i, 0, 0))],
        out_shape=[jax.ShapeDtypeStruct((b, s, CONV_W), BF16), jax.ShapeDtypeStruct((b, s, POOL_W), BF16)],
        scratch_shapes=[pltpu.VMEM((CONV_HALO + s, CONV_W), F32), pltpu.VMEM((POOL_HALO + s, POOL_W), F32),
                        pltpu.VMEM((CONV_T, CONV_W), F32)],
        compiler_params=_params("parallel"),
        name="convpool",
    )(uc, up, conv_w, conv_b[None, :], ln_g[None, :], ln_b[None, :], pool_w.astype(BF16), pool_scale[None, :])


def _gelu_tanh(x):
    c = np.float32(np.sqrt(2.0 / np.pi))
    return x * (0.5 * (1.0 + jnp.tanh(c * (x + 0.044715 * (x * x * x)))))


def _compress_kernel(kc_ref, vc_ref, pos_ref, w1_ref, b1_ref, w2_ref, b2_ref, ko_ref, vo_ref):
    n_half = kc_ref.shape[1]
    for t, (src, dst) in enumerate(((kc_ref, ko_ref), (vc_ref, vo_ref))):
        x = src[0]
        top = _dot((x + pos_ref[t, 0:1, :]).astype(BF16), w1_ref[t, 0])
        bot = _dot((x + pos_ref[t, 1:2, :]).astype(BF16), w1_ref[t, 1])
        h1 = top + pltpu.roll(bot, n_half - 1, 0) + b1_ref[t]
        out = _dot(_gelu_tanh(h1).astype(BF16), w2_ref[t]) + b2_ref[t]
        dst[0] = out.astype(BF16)


def _compress_weights(pos, w1, b1, w2, b2):
    half = CMP_BLOCK // 2
    eye = jnp.eye(NSA_KV, dtype=F32)
    pos_e = jnp.tile(pos.reshape(2, half, 1, HEAD_DIM), (1, 1, NSA_KV, 1)).reshape(2, half * NSA_KV * HEAD_DIM)
    w1r = w1.reshape(2, half, HEAD_DIM, CMP_HIDDEN)
    w1e = jnp.einsum('tldn,gh->tlgdhn', w1r, eye).reshape(2, half * NSA_KV * HEAD_DIM, NSA_KV * CMP_HIDDEN)
    b1e = jnp.tile(b1, NSA_KV)[None, :]
    w2e = jnp.einsum('nd,gh->gnhd', w2, eye).reshape(NSA_KV * CMP_HIDDEN, NSA_KV * HEAD_DIM)
    b2e = jnp.tile(b2, NSA_KV)[None, :]
    return pos_e, w1e.astype(BF16), b1e, w2e.astype(BF16), b2e


def _compress(kc, vc, kparams, vparams):
    b, nh, width = kc.shape
    stk = [jnp.stack([kp, vp]) for kp, vp in zip(kparams, vparams)]
    return pl.pallas_call(
        _compress_kernel,
        grid=(b,),
        in_specs=[pl.BlockSpec((1, nh, width), lambda i: (i, 0, 0))] * 2 + [_const_spec(a.shape) for a in stk],
        out_specs=[pl.BlockSpec((1, nh, LANES), lambda i: (i, 0, 0))] * 2,
        out_shape=[jax.ShapeDtypeStruct((b, nh, LANES), BF16)] * 2,
        compiler_params=_params("parallel"),
        name="nsa_compress",
    )(kc, vc, *stk)


def _stack_heads(q_ref, g, rows=None):
    rows = slice(None) if rows is None else rows
    return jnp.concatenate([q_ref[HEADS_PER_GROUP * g + r, 0, rows, :] for r in range(HEADS_PER_GROUP)], axis=0)


def _pair_groups(o_g0, o_g1, r, tq):
    lane = lax.broadcasted_iota(jnp.int32, (1, LANES), 1)
    return jnp.where(lane < HEAD_DIM, o_g0[r * tq:(r + 1) * tq], o_g1[r * tq:(r + 1) * tq])


def _with_ones(v, g):
    lane = lax.broadcasted_iota(jnp.int32, (1, LANES), 1)
    own = (lane < HEAD_DIM) if g == 0 else (lane >= HEAD_DIM)
    return jnp.where(own, v, jnp.ones_like(v))


def _pair_normalized(acc_g0, acc_g1, r, tq, extra=None):
    lane = lax.broadcasted_iota(jnp.int32, (1, LANES), 1)
    rows = slice(r * tq, (r + 1) * tq)
    a0, a1 = acc_g0[rows], acc_g1[rows]
    num = jnp.where(lane < HEAD_DIM, a0, a1)
    den = pltpu.roll(jnp.where(lane < HEAD_DIM, a1, a0), HEAD_DIM, 1)
    if extra is not None:
        den = jnp.where(lane < HEAD_DIM, den + extra[0][rows], den + extra[1][rows])
    return num / den


def _band_bias(tq, window, n_variants):
    q = np.arange(tq)[:, None]
    k = np.arange(window + tq)[None, :]
    out = np.full((n_variants, tq, window + tq), NEG_INF, np.float32)
    for v in range(n_variants):
        diff = min(v * tq, window) + q - k
        out[v][(diff >= 0) & (diff < window)] = 0.0
    return jnp.asarray(out)


def _add_tiled(s, bias, tq):
    return jnp.concatenate([s[r * tq:(r + 1) * tq] + bias for r in range(HEADS_PER_GROUP)], axis=0)


def _pair_rows(w):
    d = w.shape[1]
    return w.reshape(NSA_KV, HEADS_PER_GROUP, HEAD_DIM, d).transpose(1, 0, 2, 3).reshape(-1, d)


def _nsa_kernel(q_ref, kc_ref, vc_ref, vs_ref, vw_ref, gl_ref, ovt_ref, esel_ref, eg_ref,
                wbias_ref, kst_ref, kwt_ref, o_ref, m_ref, acc_ref, *, seq):
    tq = NSA_Q_TILE
    m_rows = HEADS_PER_GROUP * tq
    n_cmp = seq // CMP_STRIDE
    n_sel = seq // SLC_BLOCK
    k_top = min(N_SEL, n_sel)
    w_keys = NSA_WINDOW + tq
    qi = pl.program_id(1)
    q0 = pl.multiple_of(qi * tq, tq)
    row = lax.broadcasted_iota(jnp.int32, (m_rows, 1), 0)
    qpos = q0 + (row & (tq - 1))
    kcm = kc_ref[0]
    vcm = vc_ref[0]
    cidx = lax.broadcasted_iota(jnp.int32, (1, n_cmp), 1)
    cmp_valid = ((cidx * CMP_STRIDE + (CMP_BLOCK - 1)) <= qpos) & (cidx < n_cmp - 1)
    jidx = lax.broadcasted_iota(jnp.int32, (n_sel, 1), 0)
    qpos_t = q0 + lax.broadcasted_iota(jnp.int32, (1, tq), 1)
    forced = (jidx == 0) | (jidx == jnp.right_shift(qpos_t, int(np.log2(SLC_BLOCK))))
    causal_blk = (jidx * SLC_BLOCK) <= qpos_t
    kstart = pl.multiple_of(jnp.maximum(q0 - NSA_WINDOW, 0), tq)
    n_var = wbias_ref.shape[0]
    win_bias = wbias_ref[jnp.minimum(qi, n_var - 1)]
    diag_bias = wbias_ref[n_var - 1, :, NSA_WINDOW:]

    q_plain = [_stack_heads(q_ref, g) for g in range(NSA_KV)]

    o_win = []
    k_w = kwt_ref[0, :, pl.ds(kstart, w_keys)]
    v_w = vw_ref[0, pl.ds(kstart, w_keys), :]
    for g in range(NSA_KV):
        s_w = _add_tiled(_dot(q_plain[g], k_w), win_bias, tq)
        p_w = jnp.exp2(s_w - jnp.max(s_w, -1, keepdims=True)).astype(BF16)
        o_win.append(_dot(p_w, _with_ones(v_w, g)))

    q_aug, o_cmp = [], []
    for g in range(NSA_KV):
        qg = q_plain[g]

        s_c = _dot_nt(qg, kcm)
        sm = jnp.where(cmp_valid, s_c, NEG_INF)
        e = jnp.where(cmp_valid, jnp.exp2(sm - jnp.max(sm, -1, keepdims=True)), 0.0)
        p_c = e / jnp.maximum(jnp.sum(e, -1, keepdims=True), 1e-30)
        o_cmp.append(_dot(p_c.astype(BF16), vcm))

        psum = p_c[0:tq]
        for r in range(1, HEADS_PER_GROUP):
            psum = psum + p_c[r * tq:(r + 1) * tq]
        p_hi = psum.astype(BF16)
        p_lo = (psum - p_hi.astype(F32)).astype(BF16)
        ovt = ovt_ref[...]
        imp_t = (_dot_nt(ovt, p_hi) + _dot_nt(ovt, p_lo))[0:n_sel]
        score = jnp.where(causal_blk, imp_t + jnp.where(forced, FORCE_BONUS, 0.0), NEG_INF)
        sub = 8
        rank_rows = [jnp.zeros((sub, tq), jnp.int32) for _ in range(n_sel // sub)]
        for i in range(n_sel):
            ri = score[i:i + 1, :]
            for v in range(n_sel // sub):
                rows = score[v * sub:(v + 1) * sub]
                if v * sub > i:
                    beats = ri >= rows
                elif (v + 1) * sub - 1 <= i:
                    beats = ri > rows
                else:
                    later = jidx[v * sub:(v + 1) * sub] > i
                    beats = jnp.where(later, jnp.where(ri >= rows, 1, 0), jnp.where(ri > rows, 1, 0)) > 0
                rank_rows[v] = rank_rows[v] + jnp.where(beats, 1, 0)
        rank = jnp.concatenate(rank_rows, axis=0)
        bias_t = jnp.where(rank < k_top, 0.0, NEG_INF)
        bias_t = jnp.concatenate([bias_t, jnp.zeros((LANES - n_sel, tq), F32)], axis=0)
        bias = bias_t.T.astype(BF16)
        q_aug.append(jnp.concatenate([qg, jnp.concatenate([bias] * HEADS_PER_GROUP, axis=0)], axis=1))

    def sel_tile(kt, diagonal):
        k0 = pl.multiple_of(kt * tq, tq)
        k_aug = jnp.concatenate([kst_ref[0, :, pl.ds(k0, tq)], esel_ref[:, pl.ds(k0, tq)]], axis=0)
        v_t = vs_ref[0, pl.ds(k0, tq), :]
        for g in range(NSA_KV):
            m_i = m_ref[g]
            s = _dot(q_aug[g], k_aug)
            if diagonal:
                s = _add_tiled(s, diag_bias, tq)
            m_n = jnp.maximum(m_i, jnp.max(s, -1, keepdims=True))
            p = jnp.exp2(s - jnp.concatenate([m_n] * (tq // LANES), axis=1)).astype(BF16)
            acc_ref[g] = jnp.exp2(m_i - m_n) * acc_ref[g] + _dot(p, _with_ones(v_t, g))
            m_ref[g] = m_n

    m_ref[...] = jnp.full(m_ref.shape, NEG_INF, F32)
    acc_ref[...] = jnp.zeros(acc_ref.shape, F32)

    def sel_body(kt, carry):
        sel_tile(kt, False)
        return carry

    lax.fori_loop(0, qi, sel_body, 0)
    sel_tile(qi, True)
    o_sel = [acc_ref[g] for g in range(NSA_KV)]

    gates = _sigmoid(gl_ref[0])
    g_exp = [_split_dot(gates, eg_ref[j]) for j in range(3)]
    for r in range(HEADS_PER_GROUP):
        ls = slice(r * LANES, (r + 1) * LANES)
        y = (g_exp[0][:, ls] * _pair_groups(o_cmp[0], o_cmp[1], r, tq)
             + g_exp[1][:, ls] * _pair_normalized(o_sel[0], o_sel[1], r, tq)
             + g_exp[2][:, ls] * _pair_normalized(o_win[0], o_win[1], r, tq))
        o_ref[0, :, ls] = y.astype(BF16)


def _nsa_constants(seq):
    n_cmp = seq // CMP_STRIDE
    n_sel = seq // SLC_BLOCK
    cs = np.arange(n_cmp) * CMP_STRIDE
    ss = np.arange(n_sel) * SLC_BLOCK
    ov = (cs[:, None] <= ss[None, :] + SLC_BLOCK - 1) & (cs[:, None] + CMP_BLOCK - 1 >= ss[None, :])
    ov[n_cmp - 1, :] = False
    ovt = np.zeros((LANES, n_cmp), np.float32)
    ovt[:n_sel, :] = ov.T
    esel = np.zeros((seq, LANES), np.float32)
    esel[np.arange(seq), np.arange(seq) // SLC_BLOCK] = 1.0
    eg = np.zeros((3, LANES, NSA_HEADS * HEAD_DIM), np.float32)
    for g in range(NSA_KV):
        for r in range(HEADS_PER_GROUP):
            for j in range(3):
                c0 = r * LANES + g * HEAD_DIM
                eg[j, (g * HEADS_PER_GROUP + r) * 3 + j, c0:c0 + HEAD_DIM] = 1.0
    wbias = _band_bias(NSA_Q_TILE, NSA_WINDOW, NSA_WINDOW // NSA_Q_TILE + 1)
    return jnp.asarray(ovt, BF16), jnp.asarray(esel.T.copy(), BF16), jnp.asarray(eg, BF16), wbias


def _nsa(qn, kcm, vcm, ksw, gn, consts):
    _, b, s, _ = qn.shape
    tq = NSA_Q_TILE
    assert s % tq == 0 and s >= NSA_WINDOW + tq and s // SLC_BLOCK <= LANES
    n_cmp = s // CMP_STRIDE
    ovt, esel, eg, wbias = consts
    kv_spec = lambda j: pl.BlockSpec((1, s, LANES), lambda i, t: (i, 0, j))
    return pl.pallas_call(
        functools.partial(_nsa_kernel, seq=s),
        grid=(b, s // tq),
        in_specs=[pl.BlockSpec((NSA_HEADS, 1, tq, LANES), lambda i, t: (0, i, t, 0)),
                  pl.BlockSpec((1, n_cmp, LANES), lambda i, t: (i, 0, 0)),
                  pl.BlockSpec((1, n_cmp, LANES), lambda i, t: (i, 0, 0)),
                  kv_spec(1), kv_spec(3),
                  pl.BlockSpec((1, tq, LANES), lambda i, t: (i, t, 0)),
                  _const_spec(ovt.shape), _const_spec(esel.shape), _const_spec(eg.shape),
                  _const_spec(wbias.shape),
                  pl.BlockSpec((1, LANES, s), lambda i, t: (i, 0, 0)),
                  pl.BlockSpec((1, LANES, s), lambda i, t: (i, 0, 0))],
        out_specs=pl.BlockSpec((1, tq, NSA_HEADS * HEAD_DIM), lambda i, t: (i, t, 0)),
        out_shape=jax.ShapeDtypeStruct((b, s, NSA_HEADS * HEAD_DIM), BF16),
        scratch_shapes=[pltpu.VMEM((NSA_KV, HEADS_PER_GROUP * tq, LANES), F32),
                        pltpu.VMEM((NSA_KV, HEADS_PER_GROUP * tq, LANES), F32)],
        compiler_params=_params("parallel", "arbitrary"),
        name="nsa_attention",
    )(qn, kcm, vcm, ksw, ksw, gn, ovt, esel, eg, wbias,
      jnp.swapaxes(ksw[:, :, 0:LANES], 1, 2), jnp.swapaxes(ksw[:, :, 2 * LANES:3 * LANES], 1, 2))


def _swa_kernel(sink_ref, q_ref, k_ref, v_ref, bias_ref, o_ref):
    tq = SWA_Q_TILE
    w_keys = SWA_WINDOW + tq
    base = pl.program_id(1) * SWA_STEP
    sinks = [
        jnp.concatenate([jnp.full((tq, LANES), sink_ref[HEADS_PER_GROUP * g + r] * LOG2E, F32)
                         for r in range(HEADS_PER_GROUP)], axis=0)
        for g in range(SWA_KV)]
    for t in range(SWA_STEP // tq):
        q0 = base + t * tq
        kstart = pl.multiple_of(jnp.maximum(q0 - SWA_WINDOW, 0), tq)
        bias = bias_ref[1] if t > 0 else bias_ref[jnp.minimum(pl.program_id(1), 1)]
        k_t = k_ref[0, pl.ds(kstart, w_keys), :]
        v_t = v_ref[0, pl.ds(kstart, w_keys), :]
        accs, sink_terms = [], []
        for g in range(SWA_KV):
            s = _add_tiled(_dot_nt(_stack_heads(q_ref, g, slice(t * tq, (t + 1) * tq)), k_t), bias, tq)
            m = jnp.maximum(jnp.max(s, -1, keepdims=True), sinks[g])
            p = jnp.exp2(s - jnp.concatenate([m] * (w_keys // LANES), axis=1)).astype(BF16)
            accs.append(_dot(p, _with_ones(v_t, g)))
            sink_terms.append(jnp.exp2(sinks[g] - m))
        for r in range(HEADS_PER_GROUP):
            o_ref[0, t * tq:(t + 1) * tq, r * LANES:(r + 1) * LANES] = (
                _pair_normalized(accs[0], accs[1], r, tq, sink_terms).astype(BF16))


def _swa(qs, kvs, sinks):
    _, b, s, _ = qs.shape
    assert s % SWA_STEP == 0 and s >= SWA_WINDOW + SWA_Q_TILE
    bias = _band_bias(SWA_Q_TILE, SWA_WINDOW, SWA_WINDOW // SWA_Q_TILE + 1)
    return pl.pallas_call(
        _swa_kernel,
        grid=(b, s // SWA_STEP),
        in_specs=[pl.BlockSpec(memory_space=pltpu.SMEM),
                  pl.BlockSpec((SWA_HEADS, 1, SWA_STEP, LANES), lambda i, t: (0, i, t, 0)),
                  pl.BlockSpec((1, s, LANES), lambda i, t: (i, 0, 0)),
                  pl.BlockSpec((1, s, LANES), lambda i, t: (i, 0, 1)),
                  _const_spec(bias.shape)],
        out_specs=pl.BlockSpec((1, SWA_STEP, SWA_HEADS * HEAD_DIM), lambda i, t: (i, t, 0)),
        out_shape=jax.ShapeDtypeStruct((b, s, SWA_HEADS * HEAD_DIM), BF16),
        compiler_params=_params("parallel", "arbitrary"),
        name="swa_attention",
    )(sinks, qs, kvs, kvs, bias)


def _merge_kernel(x_ref, ya_ref, yb_ref, yc_ref, yd_ref, gt_ref, wb_ref, wo_ref, g_ref, o_ref):
    merged = None
    for n, y_ref in enumerate((ya_ref, yb_ref, yc_ref, yd_ref)):
        z = _dot(y_ref[...], wb_ref[n])
        gate = _sigmoid(gt_ref[:, n * D_MODEL:(n + 1) * D_MODEL].astype(F32))
        merged = gate * z if merged is None else merged + gate * z
    out = _dot(merged.astype(BF16), wo_ref[...])
    o_ref[...] = x_ref[...] + _rms(out, g_ref[...])


def _merge(x, ya, yb, yc, yd, gt, wb, wo, g, tm=512):
    n, d = x.shape
    row = lambda w: pl.BlockSpec((tm, w), lambda i: (i, 0))
    return pl.pallas_call(
        _merge_kernel,
        grid=(n // tm,),
        in_specs=[row(d), row(BRANCH_W), row(BRANCH_W), row(BRANCH_W), row(BRANCH_W), row(N_BRANCH * d),
                  _const_spec(wb.shape), _const_spec(wo.shape), _const_spec((1, d))],
        out_specs=row(d),
        out_shape=jax.ShapeDtypeStruct((n, d), F32),
        compiler_params=_params("parallel"),
        name="merge",
    )(x, ya, yb, yc, yd, gt, wb, wo, g[None, :])


def _memkv_kernel(mem_ref, g_ref, w_ref, o_ref):
    mn = _rms(mem_ref[0], g_ref[...]).astype(BF16)
    o_ref[0] = _dot(mn, w_ref[...]).astype(BF16)


def _memkv(mem, g, w_kv):
    b, m, d = mem.shape
    return pl.pallas_call(
        _memkv_kernel,
        grid=(b,),
        in_specs=[pl.BlockSpec((1, m, d), lambda i: (i, 0, 0)), _const_spec((1, d)), _const_spec(w_kv.shape)],
        out_specs=pl.BlockSpec((1, m, 2 * d), lambda i: (i, 0, 0)),
        out_shape=jax.ShapeDtypeStruct((b, m, 2 * d), BF16),
        compiler_params=_params("parallel"),
        name="mem_kv",
    )(mem, g[None, :], w_kv)


def _xattn_kernel(x_ref, pre_ref, wq_ref, kv_ref, wo_ref, post_ref, o_ref):
    x = x_ref[0]
    h = _rms(x, pre_ref[...]).astype(BF16)
    q = (_dot(h, wq_ref[...]) * (X_HEAD_DIM ** -0.5)).astype(BF16)
    heads = []
    for hd in range(X_HEADS):
        k_h = kv_ref[0, :, hd * X_HEAD_DIM:(hd + 1) * X_HEAD_DIM]
        v_h = kv_ref[0, :, D_MODEL + hd * X_HEAD_DIM:D_MODEL + (hd + 1) * X_HEAD_DIM]
        s = _dot_nt(q[:, hd * X_HEAD_DIM:(hd + 1) * X_HEAD_DIM], k_h)
        e = jnp.exp(s - jnp.max(s, -1, keepdims=True))
        p = e / jnp.sum(e, -1, keepdims=True)
        heads.append(_dot(p.astype(BF16), v_h).astype(BF16))
    o = jnp.concatenate(heads, axis=1)
    y = _dot(o, wo_ref[...])
    o_ref[0] = x + _rms(y, post_ref[...])


def _xattn(x, pre_g, w_q, kv, w_o, post_g, tm=512):
    b, s, d = x.shape
    m = kv.shape[1]
    return pl.pallas_call(
        _xattn_kernel,
        grid=(b, s // tm),
        in_specs=[pl.BlockSpec((1, tm, d), lambda i, t: (i, t, 0)), _const_spec((1, d)), _const_spec(w_q.shape),
                  pl.BlockSpec((1, m, 2 * d), lambda i, t: (i, 0, 0)), _const_spec(w_o.shape), _const_spec((1, d))],
        out_specs=pl.BlockSpec((1, tm, d), lambda i, t: (i, t, 0)),
        out_shape=jax.ShapeDtypeStruct((b, s, d), F32),
        compiler_params=_params("parallel", "arbitrary"),
        name="cross_attention",
    )(x, pre_g[None, :], w_q, kv, w_o, post_g[None, :])


def kernel(x, mem, positions, ffn1_pre_g, ffn1_w_in, ffn1_w_out, ffn1_post_g, mix_pre_g, w_in, conv_w, conv_b, conv_ln_g, conv_ln_b, pool_w, pool_scale, cmp_k_pos, cmp_k_w1, cmp_k_b1, cmp_k_w2, cmp_k_b2, cmp_v_pos, cmp_v_w1, cmp_v_b1, cmp_v_w2, cmp_v_b2, swa_sinks, w_branch, w_out, mix_post_g, x_pre_g, mem_g, w_xq, w_xkv, w_xo, x_post_g, ffn2_pre_g, ffn2_w_in, ffn2_w_out, ffn2_post_g):
    b, s, d = x.shape
    n = b * s
    depth = ffn1_w_in.shape[0]
    cos_t, sin_t = _rope_tables(positions)
    nsa_consts = _nsa_constants(s)
    half_w = (CMP_BLOCK // 2) * NSA_KV * HEAD_DIM
    x = x.reshape(n, d)
    for l in range(depth):
        x = _ffn(x, ffn1_pre_g[l], ffn1_w_in[l].astype(BF16), ffn1_w_out[l].astype(BF16), ffn1_post_g[l])

        uc, up, qn, kc, vc, ksw, gn, qs, kvs, gt = _inproj(x, mix_pre_g[l], cos_t, sin_t, _inproj_weights(w_in[l]))
        ya, yb = _convpool(uc.reshape(b, s, -1), up.reshape(b, s, -1), conv_w[l], conv_b[l],
                           conv_ln_g[l], conv_ln_b[l], pool_w[l], pool_scale[l])
        kcm, vcm = _compress(
            kc.reshape(b, s // CMP_STRIDE, half_w), vc.reshape(b, s // CMP_STRIDE, half_w),
            _compress_weights(cmp_k_pos[l], cmp_k_w1[l], cmp_k_b1[l], cmp_k_w2[l], cmp_k_b2[l]),
            _compress_weights(cmp_v_pos[l], cmp_v_w1[l], cmp_v_b1[l], cmp_v_w2[l], cmp_v_b2[l]))
        yc = _nsa(qn.reshape(NSA_HEADS, b, s, LANES), kcm, vcm, ksw.reshape(b, s, -1), gn.reshape(b, s, -1),
                  nsa_consts)
        yd = _swa(qs.reshape(SWA_HEADS, b, s, LANES), kvs.reshape(b, s, -1), swa_sinks[l])
        wb = jnp.stack([w_branch[l, 0], w_branch[l, 1], _pair_rows(w_branch[l, 2]), _pair_rows(w_branch[l, 3])])
        x = _merge(x, ya.reshape(n, -1), yb.reshape(n, -1), yc.reshape(n, -1), yd.reshape(n, -1), gt,
                   wb.astype(BF16), w_out[l].astype(BF16), mix_post_g[l])

        kv = _memkv(mem, mem_g[l], w_xkv[l].astype(BF16))
        x = _xattn(x.reshape(b, s, d), x_pre_g[l], w_xq[l].astype(BF16), kv, w_xo[l].astype(BF16),
                   x_post_g[l]).reshape(n, d)

        x = _ffn(x, ffn2_pre_g[l], ffn2_w_in[l].astype(BF16), ffn2_w_out[l].astype(BF16), ffn2_post_g[l])
    return x.reshape(b, s, d)
```

```python
import functools

import numpy as np
import jax
import jax.numpy as jnp
from jax import lax
from jax.experimental import pallas as pl
from jax.experimental.pallas import tpu as pltpu

F32 = jnp.float32
BF16 = jnp.bfloat16

D_MODEL = 1024
DEPTH = 4
HEAD_DIM = 64
ROPE_THETA = 10000.0
NORM_EPS = 1e-6
NEG_INF = -1e30
D_FF = 2816
CONV_W = 512
CONV_K = 31
POOL_W = 512
POOL_GROUPS = 4
POOL_WINDOWS = (2, 4, 8, 16)
NSA_HEADS = 8
NSA_KV = 2
CMP_BLOCK = 32
CMP_STRIDE = 16
CMP_HIDDEN = 256
SLC_BLOCK = 64
N_SEL = 16
FORCE_BONUS = 1e4
NSA_WINDOW = 512
SWA_HEADS = 8
SWA_KV = 2
SWA_WINDOW = 128
X_HEADS = 4
X_HEAD_DIM = D_MODEL // X_HEADS
N_BRANCH = 4
BRANCH_W = 512
IN_SIZES = (2 * CONV_W, POOL_W, NSA_HEADS * HEAD_DIM, 6 * NSA_KV * HEAD_DIM, 3 * NSA_HEADS,
            SWA_HEADS * HEAD_DIM, 2 * SWA_KV * HEAD_DIM, N_BRANCH * D_MODEL)
IN_OFFS = tuple(int(o) for o in np.cumsum((0,) + IN_SIZES))

LANES = 128
VMEM_LIMIT = 56 * 1024 * 1024
HEADS_PER_GROUP = NSA_HEADS // NSA_KV
NSA_Q_TILE = 256
SWA_Q_TILE = 128
SWA_STEP = 512
LOG2E = float(np.log2(np.e))
CONV_T = 256
CONV_HALO = 32
POOL_HALO = 16


def _params(*sem):
    return pltpu.CompilerParams(dimension_semantics=sem, vmem_limit_bytes=VMEM_LIMIT)


def _const_spec(shape):
    nd = len(shape)
    return pl.BlockSpec(shape, lambda *_: (0,) * nd, pipeline_mode=pl.Buffered(1))


def _rms(x, g):
    return x * lax.rsqrt(jnp.mean(x * x, -1, keepdims=True) + NORM_EPS) * g


def _sigmoid(x):
    return 0.5 * jnp.tanh(0.5 * x) + 0.5


def _dot(a, b):
    return jnp.dot(a, b, preferred_element_type=F32)


def _dot_nt(a, b):
    return lax.dot_general(a, b, (((1,), (1,)), ((), ())), preferred_element_type=F32)


def _split_dot(x, e):
    hi = x.astype(BF16)
    lo = (x - hi.astype(F32)).astype(BF16)
    return _dot(hi, e) + _dot(lo, e)


def _rope_kernel(pos_ref, inv_ref, sgn_ref, cos_ref, sin_ref):
    ang = pos_ref[...].astype(F32) * inv_ref[...]
    cos_ref[...] = jnp.cos(ang)
    sin_ref[...] = jnp.sin(ang) * sgn_ref[...]


def _rope_tables(positions):
    n = positions.size
    tm = 2048
    inv = ROPE_THETA ** (-jnp.arange(0, HEAD_DIM, 2, dtype=F32) / HEAD_DIM)
    inv = jnp.tile(inv, LANES // (HEAD_DIM // 2))[None, :]
    sgn = jnp.tile(jnp.concatenate([-jnp.ones(HEAD_DIM // 2, F32), jnp.ones(HEAD_DIM // 2, F32)]),
                   LANES // HEAD_DIM)[None, :]
    return pl.pallas_call(
        _rope_kernel,
        grid=(n // tm,),
        in_specs=[pl.BlockSpec((tm, 1), lambda i: (i, 0)),
                  _const_spec((1, LANES)), _const_spec((1, LANES))],
        out_specs=[pl.BlockSpec((tm, LANES), lambda i: (i, 0))] * 2,
        out_shape=[jax.ShapeDtypeStruct((n, LANES), F32)] * 2,
        compiler_params=_params("parallel"),
        name="rope_tables",
    )(positions.reshape(n, 1), inv, sgn)


def _ffn_kernel(x_ref, pre_ref, wa_ref, wb_ref, wo_ref, post_ref, o_ref):
    x = x_ref[...]
    h = _rms(x, pre_ref[...]).astype(BF16)
    a = _dot(h, wa_ref[...])
    b = _dot(h, wb_ref[...])
    t = (a * _sigmoid(a) * b).astype(BF16)
    y = _dot(t, wo_ref[...])
    o_ref[...] = x + 0.5 * _rms(y, post_ref[...])


def _ffn(x, pre_g, w_in, w_out, post_g, tm=512):
    n, d = x.shape
    f = w_out.shape[0]
    return pl.pallas_call(
        _ffn_kernel,
        grid=(n // tm,),
        in_specs=[pl.BlockSpec((tm, d), lambda i: (i, 0)),
                  _const_spec((1, d)),
                  pl.BlockSpec((d, f), lambda i: (0, 0), pipeline_mode=pl.Buffered(1)),
                  pl.BlockSpec((d, f), lambda i: (0, 1), pipeline_mode=pl.Buffered(1)),
                  _const_spec((f, d)),
                  _const_spec((1, d))],
        out_specs=pl.BlockSpec((tm, d), lambda i: (i, 0)),
        out_shape=jax.ShapeDtypeStruct((n, d), F32),
        compiler_params=_params("parallel"),
        name="ffn",
    )(x, pre_g[None, :], w_in, w_in, w_out, post_g[None, :])


def _inproj_kernel(x_ref, g_ref, cos_ref, sin_ref,
                   w_uc, w_up, w_qn, w_kn, w_gn, w_qs, w_ks, w_gt,
                   uc_ref, up_ref, qn_ref, kc_ref, vc_ref, kst_ref, kwt_ref, vsw_ref, gn_ref, qs_ref, kvs_ref, gt_ref):
    h = _rms(x_ref[...], g_ref[...]).astype(BF16)
    cos_t = cos_ref[...]
    sin_t = sin_ref[...]
    lane = lax.broadcasted_iota(jnp.int32, (1, LANES), 1)
    first_half = (lane & (HEAD_DIM // 2)) == 0

    def rope(seg):
        rot = jnp.where(first_half, pltpu.roll(seg, LANES - HEAD_DIM // 2, 1),
                        pltpu.roll(seg, HEAD_DIM // 2, 1))
        return seg * cos_t + rot * sin_t

    def seg(v, j):
        return v[:, j * LANES:(j + 1) * LANES]

    q_scale = HEAD_DIM ** -0.5 * LOG2E
    low_half = lane < HEAD_DIM

    def place_heads(q, out_ref, n_heads, n_groups):
        per_group = n_heads // n_groups
        for j in range(n_heads // 2):
            pair = rope(seg(q, j)) * q_scale
            swapped = pltpu.roll(pair, HEAD_DIM, 1)
            for k in range(2):
                head = 2 * j + k
                g = head // per_group
                src_tile = pair if k == g else swapped
                keep = low_half if g == 0 else jnp.logical_not(low_half)
                out_ref[head] = jnp.where(keep, src_tile, 0.0).astype(BF16)

    uc_ref[...] = _dot(h, w_uc[...]).astype(BF16)
    up_ref[...] = _dot(h, w_up[...]).astype(BF16)
    place_heads(_dot(h, w_qn[...]), qn_ref, NSA_HEADS, NSA_KV)
    kn = _dot(h, w_kn[...])
    kc_ref[...] = rope(seg(kn, 0))
    vc_ref[...] = seg(kn, 1)
    kst_ref[0] = rope(seg(kn, 2)).T.astype(BF16)
    kwt_ref[0] = rope(seg(kn, 4)).T.astype(BF16)
    vsw_ref[:, 0:LANES] = seg(kn, 3).astype(BF16)
    vsw_ref[:, LANES:2 * LANES] = seg(kn, 5).astype(BF16)
    gn_ref[...] = _dot(h, w_gn[...])
    place_heads(_dot(h, w_qs[...]), qs_ref, SWA_HEADS, SWA_KV)
    ks = _dot(h, w_ks[...])
    kvs_ref[:, 0:LANES] = rope(seg(ks, 0)).astype(BF16)
    kvs_ref[:, LANES:2 * LANES] = seg(ks, 1).astype(BF16)
    gt_ref[...] = _dot(h, w_gt[...]).astype(BF16)


def _inproj_weights(w_in):
    o = IN_OFFS
    w_uc = w_in[:, o[0]:o[1]]
    w_up = w_in[:, o[1]:o[2]]
    w_qn = w_in[:, o[2]:o[3]]
    w_kn = w_in[:, o[3]:o[4]]
    w_gn = jnp.pad(w_in[:, o[4]:o[5]], ((0, 0), (0, LANES - 3 * NSA_HEADS)))
    w_qs = w_in[:, o[5]:o[6]]
    w_ks = w_in[:, o[6]:o[7]]
    w_gt = w_in[:, o[7]:o[8]]
    return tuple(w.astype(BF16) for w in (w_uc, w_up, w_qn, w_kn, w_gn, w_qs, w_ks, w_gt))


def _inproj(x, g, cos_t, sin_t, weights, seq, tm=256):
    n, d = x.shape
    spb = seq // tm
    widths = [(2 * CONV_W, BF16, "row"), (POOL_W, BF16, "row"), (NSA_HEADS * LANES, BF16, "head"),
              (LANES, F32, "row"), (LANES, F32, "row"), (LANES, BF16, "tr"), (LANES, BF16, "tr"),
              (2 * LANES, BF16, "row"), (LANES, F32, "row"),
              (SWA_HEADS * LANES, BF16, "head"), (2 * LANES, BF16, "row"), (N_BRANCH * D_MODEL, BF16, "row")]
    specs = {"row": lambda w: pl.BlockSpec((tm, w), lambda i: (i, 0)),
             "head": lambda w: pl.BlockSpec((w // LANES, tm, LANES), lambda i: (0, i, 0)),
             "tr": lambda w: pl.BlockSpec((1, w, tm), lambda i: (i // spb, 0, i % spb))}
    shapes = {"row": lambda w: (n, w), "head": lambda w: (w // LANES, n, LANES), "tr": lambda w: (n // seq, w, seq)}
    return pl.pallas_call(
        _inproj_kernel,
        grid=(n // tm,),
        in_specs=[pl.BlockSpec((tm, d), lambda i: (i, 0)), _const_spec((1, d)),
                  pl.BlockSpec((tm, LANES), lambda i: (i, 0)), pl.BlockSpec((tm, LANES), lambda i: (i, 0))]
                 + [_const_spec(w.shape) for w in weights],
        out_specs=[specs[kind](w) for w, _, kind in widths],
        out_shape=[jax.ShapeDtypeStruct(shapes[kind](w), dt) for w, dt, kind in widths],
        compiler_params=_params("parallel"),
        name="inproj",
    )(x, g[None, :], cos_t, sin_t, *weights)


def _convpool_kernel(uc_ref, up_ref, cw_ref, cb_ref, lng_ref, lnb_ref, pw_ref, psc_ref,
                     ya_ref, yb_ref, vbuf, pbuf, cbuf, *, seq):
    t_c = CONV_T
    vbuf[0:CONV_HALO, :] = jnp.zeros((CONV_HALO, CONV_W), F32)
    pbuf[0:POOL_HALO, :] = jnp.zeros((POOL_HALO, POOL_W), F32)

    def fill(c, carry):
        r0 = pl.multiple_of(c * t_c, t_c)
        u = uc_ref[0, pl.ds(r0, t_c), :].astype(F32)
        vbuf[pl.ds(CONV_HALO + r0, t_c), :] = u[:, :CONV_W] * _sigmoid(u[:, CONV_W:])
        pbuf[pl.ds(POOL_HALO + r0, t_c), :] = up_ref[0, pl.ds(r0, t_c), :].astype(F32)
        return carry

    lax.fori_loop(0, seq // t_c, fill, 0)

    cg = POOL_W // POOL_GROUPS

    def mix(c, carry):
        r0 = pl.multiple_of(c * t_c, t_c)
        for lt in range(CONV_W // LANES):
            ls = slice(lt * LANES, (lt + 1) * LANES)
            xext = vbuf[pl.ds(r0, t_c + CONV_HALO), ls]
            acc = jnp.broadcast_to(cb_ref[:, ls], (t_c, LANES))
            for r in range(8):
                rolled = xext if r == 0 else pltpu.roll(xext, r, 0)
                for a in range(CONV_HALO // 8):
                    shift = 8 * a + r
                    if shift > CONV_K - 1:
                        continue
                    k = CONV_K - 1 - shift
                    base = CONV_HALO - 8 * a
                    acc = acc + rolled[base:base + t_c, :] * cw_ref[k:k + 1, ls]
            cbuf[:, ls] = acc
        y = cbuf[...]
        mu = jnp.mean(y, -1, keepdims=True)
        yc = y - mu
        var = jnp.mean(yc * yc, -1, keepdims=True)
        z = yc * lax.rsqrt(var + NORM_EPS) * lng_ref[...] + lnb_ref[...]
        ya_ref[0, pl.ds(r0, t_c), :] = (z * _sigmoid(z)).astype(BF16)

        pext = pbuf[pl.ds(r0, t_c + POOL_HALO), :]
        e2 = pext + pltpu.roll(pext, 1, 0)
        e4 = e2[:, cg:] + pltpu.roll(e2[:, cg:], 2, 0)
        e8 = e4[:, cg:] + pltpu.roll(e4[:, cg:], 4, 0)
        e16 = e8[:, cg:] + pltpu.roll(e8[:, cg:], 8, 0)
        tpos = r0 + lax.broadcasted_iota(jnp.int32, (t_c, 1), 0)
        for g, (w, e) in enumerate(zip(POOL_WINDOWS, (e2, e4, e8, e16))):
            cnt = jnp.minimum(tpos + 1, w).astype(F32)
            v_g = pext[POOL_HALO:, g * cg:(g + 1) * cg]
            dlt = e[POOL_HALO:, :cg] / cnt - v_g
            yg = _dot(dlt.astype(BF16), pw_ref[g]) * psc_ref[:, g * cg:(g + 1) * cg]
            yb_ref[0, pl.ds(r0, t_c), g * cg:(g + 1) * cg] = yg.astype(BF16)
        return carry

    lax.fori_loop(0, seq // t_c, mix, 0)


def _convpool(uc, up, conv_w, conv_b, ln_g, ln_b, pool_w, pool_scale):
    b, s, _ = uc.shape
    return pl.pallas_call(
        functools.partial(_convpool_kernel, seq=s),
        grid=(b,),
        in_specs=[pl.BlockSpec((1, s, 2 * CONV_W), lambda i: (i, 0, 0)),
                  pl.BlockSpec((1, s, POOL_W), lambda i: (i, 0, 0)),
                  _const_spec((CONV_K, CONV_W)), _const_spec((1, CONV_W)),
                  _const_spec((1, CONV_W)), _const_spec((1, CONV_W)),
                  _const_spec(pool_w.shape), _const_spec((1, POOL_W))],
        out_specs=[pl.BlockSpec((1, s, CONV_W), lambda i: (i, 0, 0)),
                   pl.BlockSpec((1, s, POOL_W), lambda i: (i, 0, 0))],
        out_shape=[jax.ShapeDtypeStruct((b, s, CONV_W), BF16), jax.ShapeDtypeStruct((b, s, POOL_W), BF16)],
        scratch_shapes=[pltpu.VMEM((CONV_HALO + s, CONV_W), F32), pltpu.VMEM((POOL_HALO + s, POOL_W), F32),
                        pltpu.VMEM((CONV_T, CONV_W), F32)],
        compiler_params=_params("parallel"),
        name="convpool",
    )(uc, up, conv_w, conv_b[None, :], ln_g[None, :], ln_b[None, :], pool_w.astype(BF16), pool_scale[None, :])


def _gelu_tanh(x):
    c = np.float32(np.sqrt(2.0 / np.pi))
    return x * (0.5 * (1.0 + jnp.tanh(c * (x + 0.044715 * (x * x * x)))))


def _compress_kernel(kc_ref, vc_ref, pos_ref, w1_ref, b1_ref, w2_ref, b2_ref, ko_ref, vo_ref):
    n_half = kc_ref.shape[1] // CMP_STRIDE
    for t, (src, dst) in enumerate(((kc_ref, ko_ref), (vc_ref, vo_ref))):
        x = jnp.concatenate([src[0, pl.ds(l, n_half, stride=CMP_STRIDE), :] for l in range(CMP_STRIDE)], axis=1)
        top = _dot((x + pos_ref[t, 0:1, :]).astype(BF16), w1_ref[t, 0])
        bot = _dot((x + pos_ref[t, 1:2, :]).astype(BF16), w1_ref[t, 1])
        h1 = top + pltpu.roll(bot, n_half - 1, 0) + b1_ref[t]
        out = _dot(_gelu_tanh(h1).astype(BF16), w2_ref[t]) + b2_ref[t]
        dst[0] = out.astype(BF16)


def _compress_weights(pos, w1, b1, w2, b2):
    half = CMP_BLOCK // 2
    eye = jnp.eye(NSA_KV, dtype=F32)
    pos_e = jnp.tile(pos.reshape(2, half, 1, HEAD_DIM), (1, 1, NSA_KV, 1)).reshape(2, half * NSA_KV * HEAD_DIM)
    w1r = w1.reshape(2, half, HEAD_DIM, CMP_HIDDEN)
    w1e = jnp.einsum('tldn,gh->tlgdhn', w1r, eye).reshape(2, half * NSA_KV * HEAD_DIM, NSA_KV * CMP_HIDDEN)
    b1e = jnp.tile(b1, NSA_KV)[None, :]
    w2e = jnp.einsum('nd,gh->gnhd', w2, eye).reshape(NSA_KV * CMP_HIDDEN, NSA_KV * HEAD_DIM)
    b2e = jnp.tile(b2, NSA_KV)[None, :]
    return pos_e, w1e.astype(BF16), b1e, w2e.astype(BF16), b2e


def _compress(kc, vc, kparams, vparams):
    b, s, width = kc.shape
    nh = s // CMP_STRIDE
    stk = [jnp.stack([kp, vp]) for kp, vp in zip(kparams, vparams)]
    return pl.pallas_call(
        _compress_kernel,
        grid=(b,),
        in_specs=[pl.BlockSpec((1, s, width), lambda i: (i, 0, 0))] * 2 + [_const_spec(a.shape) for a in stk],
        out_specs=[pl.BlockSpec((1, nh, LANES), lambda i: (i, 0, 0))] * 2,
        out_shape=[jax.ShapeDtypeStruct((b, nh, LANES), BF16)] * 2,
        compiler_params=_params("parallel"),
        name="nsa_compress",
    )(kc, vc, *stk)


def _stack_heads(q_ref, g, rows=None):
    rows = slice(None) if rows is None else rows
    return jnp.concatenate([q_ref[HEADS_PER_GROUP * g + r, 0, rows, :] for r in range(HEADS_PER_GROUP)], axis=0)


def _pair_groups(o_g0, o_g1, r, tq):
    lane = lax.broadcasted_iota(jnp.int32, (1, LANES), 1)
    return jnp.where(lane < HEAD_DIM, o_g0[r * tq:(r + 1) * tq], o_g1[r * tq:(r + 1) * tq])


def _with_ones(v, g):
    lane = lax.broadcasted_iota(jnp.int32, (1, LANES), 1)
    own = (lane < HEAD_DIM) if g == 0 else (lane >= HEAD_DIM)
    return jnp.where(own, v, jnp.ones_like(v))


def _pair_normalized(acc_g0, acc_g1, r, tq, extra=None):
    lane = lax.broadcasted_iota(jnp.int32, (1, LANES), 1)
    rows = slice(r * tq, (r + 1) * tq)
    a0, a1 = acc_g0[rows], acc_g1[rows]
    num = jnp.where(lane < HEAD_DIM, a0, a1)
    den = pltpu.roll(jnp.where(lane < HEAD_DIM, a1, a0), HEAD_DIM, 1)
    if extra is not None:
        den = jnp.where(lane < HEAD_DIM, den + extra[0][rows], den + extra[1][rows])
    return num / den


def _band_bias(tq, window, n_variants):
    q = np.arange(tq)[:, None]
    k = np.arange(window + tq)[None, :]
    out = np.full((n_variants, tq, window + tq), NEG_INF, np.float32)
    for v in range(n_variants):
        diff = min(v * tq, window) + q - k
        out[v][(diff >= 0) & (diff < window)] = 0.0
    return jnp.asarray(out)


def _add_tiled(s, bias, tq):
    return jnp.concatenate([s[r * tq:(r + 1) * tq] + bias for r in range(HEADS_PER_GROUP)], axis=0)


def _pair_rows(w):
    d = w.shape[1]
    return w.reshape(NSA_KV, HEADS_PER_GROUP, HEAD_DIM, d).transpose(1, 0, 2, 3).reshape(-1, d)


def _nsa_kernel(q_ref, kc_ref, vc_ref, vs_ref, vw_ref, gl_ref, ovt_ref, esel_ref, eg_ref,
                wbias_ref, kst_ref, kwt_ref, o_ref, m_ref, acc_ref, *, seq):
    tq = NSA_Q_TILE
    m_rows = HEADS_PER_GROUP * tq
    n_cmp = seq // CMP_STRIDE
    n_sel = seq // SLC_BLOCK
    k_top = min(N_SEL, n_sel)
    w_keys = NSA_WINDOW + tq
    qi = pl.program_id(1)
    q0 = pl.multiple_of(qi * tq, tq)
    row = lax.broadcasted_iota(jnp.int32, (m_rows, 1), 0)
    qpos = q0 + (row & (tq - 1))
    kcm = kc_ref[0]
    vcm = vc_ref[0]
    cidx = lax.broadcasted_iota(jnp.int32, (1, n_cmp), 1)
    cmp_valid = ((cidx * CMP_STRIDE + (CMP_BLOCK - 1)) <= qpos) & (cidx < n_cmp - 1)
    jidx = lax.broadcasted_iota(jnp.int32, (n_sel, 1), 0)
    qpos_t = q0 + lax.broadcasted_iota(jnp.int32, (1, tq), 1)
    forced = (jidx == 0) | (jidx == jnp.right_shift(qpos_t, int(np.log2(SLC_BLOCK))))
    causal_blk = (jidx * SLC_BLOCK) <= qpos_t
    kstart = pl.multiple_of(jnp.maximum(q0 - NSA_WINDOW, 0), tq)
    n_var = wbias_ref.shape[0]
    win_bias = wbias_ref[jnp.minimum(qi, n_var - 1)]
    diag_bias = wbias_ref[n_var - 1, :, NSA_WINDOW:]

    q_plain = [_stack_heads(q_ref, g) for g in range(NSA_KV)]

    o_win = []
    k_w = kwt_ref[0, :, pl.ds(kstart, w_keys)]
    v_w = vw_ref[0, pl.ds(kstart, w_keys), :]
    for g in range(NSA_KV):
        s_w = _add_tiled(_dot(q_plain[g], k_w), win_bias, tq)
        p_w = jnp.exp2(s_w - jnp.max(s_w, -1, keepdims=True)).astype(BF16)
        o_win.append(_dot(p_w, _with_ones(v_w, g)))

    q_aug, o_cmp = [], []
    for g in range(NSA_KV):
        qg = q_plain[g]

        s_c = _dot_nt(qg, kcm)
        sm = jnp.where(cmp_valid, s_c, NEG_INF)
        e = jnp.where(cmp_valid, jnp.exp2(sm - jnp.max(sm, -1, keepdims=True)), 0.0)
        p_c = e / jnp.maximum(jnp.sum(e, -1, keepdims=True), 1e-30)
        o_cmp.append(_dot(p_c.astype(BF16), vcm))

        psum = p_c[0:tq]
        for r in range(1, HEADS_PER_GROUP):
            psum = psum + p_c[r * tq:(r + 1) * tq]
        p_hi = psum.astype(BF16)
        p_lo = (psum - p_hi.astype(F32)).astype(BF16)
        ovt = ovt_ref[...]
        imp_t = (_dot_nt(ovt, p_hi) + _dot_nt(ovt, p_lo))[0:n_sel]
        score = jnp.where(causal_blk, imp_t + jnp.where(forced, FORCE_BONUS, 0.0), NEG_INF)
        sub = 8
        rank_rows = [jnp.zeros((sub, tq), jnp.int32) for _ in range(n_sel // sub)]
        for i in range(n_sel):
            ri = score[i:i + 1, :]
            for v in range(n_sel // sub):
                rows = score[v * sub:(v + 1) * sub]
                if v * sub > i:
                    beats = ri >= rows
                elif (v + 1) * sub - 1 <= i:
                    beats = ri > rows
                else:
                    later = jidx[v * sub:(v + 1) * sub] > i
                    beats = jnp.where(later, jnp.where(ri >= rows, 1, 0), jnp.where(ri > rows, 1, 0)) > 0
                rank_rows[v] = rank_rows[v] + jnp.where(beats, 1, 0)
        rank = jnp.concatenate(rank_rows, axis=0)
        bias_t = jnp.where(rank < k_top, 0.0, NEG_INF)
        bias_t = jnp.concatenate([bias_t, jnp.zeros((LANES - n_sel, tq), F32)], axis=0)
        bias = bias_t.T.astype(BF16)
        q_aug.append(jnp.concatenate([qg, jnp.concatenate([bias] * HEADS_PER_GROUP, axis=0)], axis=1))

    def sel_tile(kt, diagonal):
        k0 = pl.multiple_of(kt * tq, tq)
        k_aug = jnp.concatenate([kst_ref[0, :, pl.ds(k0, tq)], esel_ref[:, pl.ds(k0, tq)]], axis=0)
        v_t = vs_ref[0, pl.ds(k0, tq), :]
        for g in range(NSA_KV):
            m_i = m_ref[g]
            s = _dot(q_aug[g], k_aug)
            if diagonal:
                s = _add_tiled(s, diag_bias, tq)
            m_n = jnp.maximum(m_i, jnp.max(s, -1, keepdims=True))
            p = jnp.exp2(s - jnp.concatenate([m_n] * (tq // LANES), axis=1)).astype(BF16)
            acc_ref[g] = jnp.exp2(m_i - m_n) * acc_ref[g] + _dot(p, _with_ones(v_t, g))
            m_ref[g] = m_n

    m_ref[...] = jnp.full(m_ref.shape, NEG_INF, F32)
    acc_ref[...] = jnp.zeros(acc_ref.shape, F32)

    def sel_group(width):
        def body(j, first):
            for u in range(width):
                sel_tile(first + width * j + u, False)
            return first
        return body

    done = 0
    for width in (4, 2, 1):
        trips = (qi - done) // width
        lax.fori_loop(0, trips, sel_group(width), done)
        done = done + trips * width
    sel_tile(qi, True)
    o_sel = [acc_ref[g] for g in range(NSA_KV)]

    gates = _sigmoid(gl_ref[0])
    g_exp = [_split_dot(gates, eg_ref[j]) for j in range(3)]
    for r in range(HEADS_PER_GROUP):
        ls = slice(r * LANES, (r + 1) * LANES)
        y = (g_exp[0][:, ls] * _pair_groups(o_cmp[0], o_cmp[1], r, tq)
             + g_exp[1][:, ls] * _pair_normalized(o_sel[0], o_sel[1], r, tq)
             + g_exp[2][:, ls] * _pair_normalized(o_win[0], o_win[1], r, tq))
        o_ref[0, :, ls] = y.astype(BF16)


def _nsa_constants(seq):
    n_cmp = seq // CMP_STRIDE
    n_sel = seq // SLC_BLOCK
    cs = np.arange(n_cmp) * CMP_STRIDE
    ss = np.arange(n_sel) * SLC_BLOCK
    ov = (cs[:, None] <= ss[None, :] + SLC_BLOCK - 1) & (cs[:, None] + CMP_BLOCK - 1 >= ss[None, :])
    ov[n_cmp - 1, :] = False
    ovt = np.zeros((LANES, n_cmp), np.float32)
    ovt[:n_sel, :] = ov.T
    esel = np.zeros((seq, LANES), np.float32)
    esel[np.arange(seq), np.arange(seq) // SLC_BLOCK] = 1.0
    eg = np.zeros((3, LANES, NSA_HEADS * HEAD_DIM), np.float32)
    for g in range(NSA_KV):
        for r in range(HEADS_PER_GROUP):
            for j in range(3):
                c0 = r * LANES + g * HEAD_DIM
                eg[j, (g * HEADS_PER_GROUP + r) * 3 + j, c0:c0 + HEAD_DIM] = 1.0
    wbias = _band_bias(NSA_Q_TILE, NSA_WINDOW, NSA_WINDOW // NSA_Q_TILE + 1)
    return jnp.asarray(ovt, BF16), jnp.asarray(esel.T.copy(), BF16), jnp.asarray(eg, BF16), wbias


def _nsa(qn, kcm, vcm, kst, kwt, vsw, gn, consts):
    _, b, s, _ = qn.shape
    tq = NSA_Q_TILE
    assert s % tq == 0 and s >= NSA_WINDOW + tq and s // SLC_BLOCK <= LANES
    n_cmp = s // CMP_STRIDE
    ovt, esel, eg, wbias = consts
    kv_spec = lambda j: pl.BlockSpec((1, s, LANES), lambda i, t: (i, 0, j))
    return pl.pallas_call(
        functools.partial(_nsa_kernel, seq=s),
        grid=(b, s // tq),
        in_specs=[pl.BlockSpec((NSA_HEADS, 1, tq, LANES), lambda i, t: (0, i, t, 0)),
                  pl.BlockSpec((1, n_cmp, LANES), lambda i, t: (i, 0, 0)),
                  pl.BlockSpec((1, n_cmp, LANES), lambda i, t: (i, 0, 0)),
                  kv_spec(0), kv_spec(1),
                  pl.BlockSpec((1, tq, LANES), lambda i, t: (i, t, 0)),
                  _const_spec(ovt.shape), _const_spec(esel.shape), _const_spec(eg.shape),
                  _const_spec(wbias.shape),
                  pl.BlockSpec((1, LANES, s), lambda i, t: (i, 0, 0)),
                  pl.BlockSpec((1, LANES, s), lambda i, t: (i, 0, 0))],
        out_specs=pl.BlockSpec((1, tq, NSA_HEADS * HEAD_DIM), lambda i, t: (i, t, 0)),
        out_shape=jax.ShapeDtypeStruct((b, s, NSA_HEADS * HEAD_DIM), BF16),
        scratch_shapes=[pltpu.VMEM((NSA_KV, HEADS_PER_GROUP * tq, LANES), F32),
                        pltpu.VMEM((NSA_KV, HEADS_PER_GROUP * tq, LANES), F32)],
        compiler_params=_params("parallel", "arbitrary"),
        name="nsa_attention",
    )(qn, kcm, vcm, vsw, vsw, gn, ovt, esel, eg, wbias, kst, kwt)


def _swa_kernel(sink_ref, q_ref, k_ref, v_ref, bias_ref, o_ref):
    tq = SWA_Q_TILE
    w_keys = SWA_WINDOW + tq
    base = pl.program_id(1) * SWA_STEP
    sinks = [
        jnp.concatenate([jnp.full((tq, LANES), sink_ref[HEADS_PER_GROUP * g + r] * LOG2E, F32)
                         for r in range(HEADS_PER_GROUP)], axis=0)
        for g in range(SWA_KV)]
    for t in range(SWA_STEP // tq):
        q0 = base + t * tq
        kstart = pl.multiple_of(jnp.maximum(q0 - SWA_WINDOW, 0), tq)
        bias = bias_ref[1] if t > 0 else bias_ref[jnp.minimum(pl.program_id(1), 1)]
        k_t = k_ref[0, pl.ds(kstart, w_keys), :]
        v_t = v_ref[0, pl.ds(kstart, w_keys), :]
        accs, sink_terms = [], []
        for g in range(SWA_KV):
            s = _add_tiled(_dot_nt(_stack_heads(q_ref, g, slice(t * tq, (t + 1) * tq)), k_t), bias, tq)
            m = jnp.maximum(jnp.max(s, -1, keepdims=True), sinks[g])
            p = jnp.exp2(s - jnp.concatenate([m] * (w_keys // LANES), axis=1)).astype(BF16)
            accs.append(_dot(p, _with_ones(v_t, g)))
            sink_terms.append(jnp.exp2(sinks[g] - m))
        for r in range(HEADS_PER_GROUP):
            o_ref[0, t * tq:(t + 1) * tq, r * LANES:(r + 1) * LANES] = (
                _pair_normalized(accs[0], accs[1], r, tq, sink_terms).astype(BF16))


def _swa(qs, kvs, sinks):
    _, b, s, _ = qs.shape
    assert s % SWA_STEP == 0 and s >= SWA_WINDOW + SWA_Q_TILE
    bias = _band_bias(SWA_Q_TILE, SWA_WINDOW, SWA_WINDOW // SWA_Q_TILE + 1)
    return pl.pallas_call(
        _swa_kernel,
        grid=(b, s // SWA_STEP),
        in_specs=[pl.BlockSpec(memory_space=pltpu.SMEM),
                  pl.BlockSpec((SWA_HEADS, 1, SWA_STEP, LANES), lambda i, t: (0, i, t, 0)),
                  pl.BlockSpec((1, s, LANES), lambda i, t: (i, 0, 0)),
                  pl.BlockSpec((1, s, LANES), lambda i, t: (i, 0, 1)),
                  _const_spec(bias.shape)],
        out_specs=pl.BlockSpec((1, SWA_STEP, SWA_HEADS * HEAD_DIM), lambda i, t: (i, t, 0)),
        out_shape=jax.ShapeDtypeStruct((b, s, SWA_HEADS * HEAD_DIM), BF16),
        compiler_params=_params("parallel", "arbitrary"),
        name="swa_attention",
    )(sinks, qs, kvs, kvs, bias)


def _merge_kernel(x_ref, ya_ref, yb_ref, yc_ref, yd_ref, gt_ref, wb_ref, wo_ref, g_ref, o_ref):
    merged = None
    for n, y_ref in enumerate((ya_ref, yb_ref, yc_ref, yd_ref)):
        z = _dot(y_ref[...], wb_ref[n])
        gate = _sigmoid(gt_ref[:, n * D_MODEL:(n + 1) * D_MODEL].astype(F32))
        merged = gate * z if merged is None else merged + gate * z
    out = _dot(merged.astype(BF16), wo_ref[...])
    o_ref[...] = x_ref[...] + _rms(out, g_ref[...])


def _merge(x, ya, yb, yc, yd, gt, wb, wo, g, tm=512):
    n, d = x.shape
    row = lambda w: pl.BlockSpec((tm, w), lambda i: (i, 0))
    return pl.pallas_call(
        _merge_kernel,
        grid=(n // tm,),
        in_specs=[row(d), row(BRANCH_W), row(BRANCH_W), row(BRANCH_W), row(BRANCH_W), row(N_BRANCH * d),
                  _const_spec(wb.shape), _const_spec(wo.shape), _const_spec((1, d))],
        out_specs=row(d),
        out_shape=jax.ShapeDtypeStruct((n, d), F32),
        compiler_params=_params("parallel"),
        name="merge",
    )(x, ya, yb, yc, yd, gt, wb, wo, g[None, :])


def _memkv_kernel(mem_ref, g_ref, w_ref, o_ref):
    mn = _rms(mem_ref[0], g_ref[...]).astype(BF16)
    o_ref[0] = _dot(mn, w_ref[...]).astype(BF16)


def _memkv(mem, g, w_kv):
    b, m, d = mem.shape
    return pl.pallas_call(
        _memkv_kernel,
        grid=(b,),
        in_specs=[pl.BlockSpec((1, m, d), lambda i: (i, 0, 0)), _const_spec((1, d)), _const_spec(w_kv.shape)],
        out_specs=pl.BlockSpec((1, m, 2 * d), lambda i: (i, 0, 0)),
        out_shape=jax.ShapeDtypeStruct((b, m, 2 * d), BF16),
        compiler_params=_params("parallel"),
        name="mem_kv",
    )(mem, g[None, :], w_kv)


def _xattn_kernel(x_ref, pre_ref, wq_ref, kv_ref, wo_ref, post_ref, o_ref):
    x = x_ref[0]
    h = _rms(x, pre_ref[...]).astype(BF16)
    q = (_dot(h, wq_ref[...]) * (X_HEAD_DIM ** -0.5)).astype(BF16)
    heads = []
    for hd in range(X_HEADS):
        k_h = kv_ref[0, :, hd * X_HEAD_DIM:(hd + 1) * X_HEAD_DIM]
        v_h = kv_ref[0, :, D_MODEL + hd * X_HEAD_DIM:D_MODEL + (hd + 1) * X_HEAD_DIM]
        s = _dot_nt(q[:, hd * X_HEAD_DIM:(hd + 1) * X_HEAD_DIM], k_h)
        e = jnp.exp(s - jnp.max(s, -1, keepdims=True))
        p = e / jnp.sum(e, -1, keepdims=True)
        heads.append(_dot(p.astype(BF16), v_h).astype(BF16))
    o = jnp.concatenate(heads, axis=1)
    y = _dot(o, wo_ref[...])
    o_ref[0] = x + _rms(y, post_ref[...])


def _xattn(x, pre_g, w_q, kv, w_o, post_g, tm=1024):
    b, s, d = x.shape
    m = kv.shape[1]
    return pl.pallas_call(
        _xattn_kernel,
        grid=(b, s // tm),
        in_specs=[pl.BlockSpec((1, tm, d), lambda i, t: (i, t, 0)), _const_spec((1, d)), _const_spec(w_q.shape),
                  pl.BlockSpec((1, m, 2 * d), lambda i, t: (i, 0, 0)), _const_spec(w_o.shape), _const_spec((1, d))],
        out_specs=pl.BlockSpec((1, tm, d), lambda i, t: (i, t, 0)),
        out_shape=jax.ShapeDtypeStruct((b, s, d), F32),
        compiler_params=_params("parallel", "arbitrary"),
        name="cross_attention",
    )(x, pre_g[None, :], w_q, kv, w_o, post_g[None, :])


def kernel(x, mem, positions, ffn1_pre_g, ffn1_w_in, ffn1_w_out, ffn1_post_g, mix_pre_g, w_in, conv_w, conv_b, conv_ln_g, conv_ln_b, pool_w, pool_scale, cmp_k_pos, cmp_k_w1, cmp_k_b1, cmp_k_w2, cmp_k_b2, cmp_v_pos, cmp_v_w1, cmp_v_b1, cmp_v_w2, cmp_v_b2, swa_sinks, w_branch, w_out, mix_post_g, x_pre_g, mem_g, w_xq, w_xkv, w_xo, x_post_g, ffn2_pre_g, ffn2_w_in, ffn2_w_out, ffn2_post_g):
    b, s, d = x.shape
    n = b * s
    depth = ffn1_w_in.shape[0]
    cos_t, sin_t = _rope_tables(positions)
    nsa_consts = _nsa_constants(s)
    x = x.reshape(n, d)
    for l in range(depth):
        x = _ffn(x, ffn1_pre_g[l], ffn1_w_in[l].astype(BF16), ffn1_w_out[l].astype(BF16), ffn1_post_g[l])

        uc, up, qn, kc, vc, kst, kwt, vsw, gn, qs, kvs, gt = _inproj(x, mix_pre_g[l], cos_t, sin_t,
                                                                     _inproj_weights(w_in[l]), s)
        ya, yb = _convpool(uc.reshape(b, s, -1), up.reshape(b, s, -1), conv_w[l], conv_b[l],
                           conv_ln_g[l], conv_ln_b[l], pool_w[l], pool_scale[l])
        kcm, vcm = _compress(
            kc.reshape(b, s, LANES), vc.reshape(b, s, LANES),
            _compress_weights(cmp_k_pos[l], cmp_k_w1[l], cmp_k_b1[l], cmp_k_w2[l], cmp_k_b2[l]),
            _compress_weights(cmp_v_pos[l], cmp_v_w1[l], cmp_v_b1[l], cmp_v_w2[l], cmp_v_b2[l]))
        yc = _nsa(qn.reshape(NSA_HEADS, b, s, LANES), kcm, vcm, kst, kwt, vsw.reshape(b, s, -1),
                  gn.reshape(b, s, -1), nsa_consts)
        yd = _swa(qs.reshape(SWA_HEADS, b, s, LANES), kvs.reshape(b, s, -1), swa_sinks[l])
        wb = jnp.stack([w_branch[l, 0], w_branch[l, 1], _pair_rows(w_branch[l, 2]), _pair_rows(w_branch[l, 3])])
        x = _merge(x, ya.reshape(n, -1), yb.reshape(n, -1), yc.reshape(n, -1), yd.reshape(n, -1), gt,
                   wb.astype(BF16), w_out[l].astype(BF16), mix_post_g[l])

        kv = _memkv(mem, mem_g[l], w_xkv[l].astype(BF16))
        x = _xattn(x.reshape(b, s, d), x_pre_g[l], w_xq[l].astype(BF16), kv, w_xo[l].astype(BF16),
                   x_post_g[l]).reshape(n, d)

        x = _ffn(x, ffn2_pre_g[l], ffn2_w_in[l].astype(BF16), ffn2_w_out[l].astype(BF16), ffn2_post_g[l])
    return x.reshape(b, s, d)
```

```python
import functools

import numpy as np
import jax
import jax.numpy as jnp
from jax import lax
from jax.experimental import pallas as pl
from jax.experimental.pallas import tpu as pltpu

F32 = jnp.float32
BF16 = jnp.bfloat16

D_MODEL = 1024
DEPTH = 4
HEAD_DIM = 64
ROPE_THETA = 10000.0
NORM_EPS = 1e-6
NEG_INF = -1e30
D_FF = 2816
CONV_W = 512
CONV_K = 31
POOL_W = 512
POOL_GROUPS = 4
POOL_WINDOWS = (2, 4, 8, 16)
NSA_HEADS = 8
NSA_KV = 2
CMP_BLOCK = 32
CMP_STRIDE = 16
CMP_HIDDEN = 256
SLC_BLOCK = 64
N_SEL = 16
FORCE_BONUS = 1e4
NSA_WINDOW = 512
SWA_HEADS = 8
SWA_KV = 2
SWA_WINDOW = 128
X_HEADS = 4
X_HEAD_DIM = D_MODEL // X_HEADS
N_BRANCH = 4
BRANCH_W = 512
IN_SIZES = (2 * CONV_W, POOL_W, NSA_HEADS * HEAD_DIM, 6 * NSA_KV * HEAD_DIM, 3 * NSA_HEADS,
            SWA_HEADS * HEAD_DIM, 2 * SWA_KV * HEAD_DIM, N_BRANCH * D_MODEL)
IN_OFFS = tuple(int(o) for o in np.cumsum((0,) + IN_SIZES))

LANES = 128
VMEM_LIMIT = 56 * 1024 * 1024
HEADS_PER_GROUP = NSA_HEADS // NSA_KV
NSA_Q_TILE = 256
SWA_Q_TILE = 128
SWA_STEP = 256
LOG2E = float(np.log2(np.e))
CONV_T = 256
CONV_HALO = 32
POOL_HALO = 16


def _params(*sem):
    return pltpu.CompilerParams(dimension_semantics=sem, vmem_limit_bytes=VMEM_LIMIT)


def _const_spec(shape):
    nd = len(shape)
    return pl.BlockSpec(shape, lambda *_: (0,) * nd, pipeline_mode=pl.Buffered(1))


def _rms(x, g):
    return x * lax.rsqrt(jnp.mean(x * x, -1, keepdims=True) + NORM_EPS) * g


def _sigmoid(x):
    return 0.5 * jnp.tanh(0.5 * x) + 0.5


def _dot(a, b):
    return jnp.dot(a, b, preferred_element_type=F32)


def _dot_nt(a, b):
    return lax.dot_general(a, b, (((1,), (1,)), ((), ())), preferred_element_type=F32)


def _split_dot(x, e):
    hi = x.astype(BF16)
    lo = (x - hi.astype(F32)).astype(BF16)
    return _dot(hi, e) + _dot(lo, e)


def _rope_kernel(pos_ref, inv_ref, sgn_ref, cos_ref, sin_ref):
    ang = pos_ref[...].astype(F32) * inv_ref[...]
    cos_ref[...] = jnp.cos(ang)
    sin_ref[...] = jnp.sin(ang) * sgn_ref[...]


def _rope_tables(positions):
    n = positions.size
    tm = 2048
    inv = ROPE_THETA ** (-jnp.arange(0, HEAD_DIM, 2, dtype=F32) / HEAD_DIM)
    inv = jnp.tile(inv, LANES // (HEAD_DIM // 2))[None, :]
    sgn = jnp.tile(jnp.concatenate([-jnp.ones(HEAD_DIM // 2, F32), jnp.ones(HEAD_DIM // 2, F32)]),
                   LANES // HEAD_DIM)[None, :]
    return pl.pallas_call(
        _rope_kernel,
        grid=(n // tm,),
        in_specs=[pl.BlockSpec((tm, 1), lambda i: (i, 0)),
                  _const_spec((1, LANES)), _const_spec((1, LANES))],
        out_specs=[pl.BlockSpec((tm, LANES), lambda i: (i, 0))] * 2,
        out_shape=[jax.ShapeDtypeStruct((n, LANES), F32)] * 2,
        compiler_params=_params("parallel"),
        name="rope_tables",
    )(positions.reshape(n, 1), inv, sgn)


def _ffn_kernel(x_ref, pre_ref, wa_ref, wb_ref, wo_ref, post_ref, o_ref):
    x = x_ref[...]
    h = _rms(x, pre_ref[...]).astype(BF16)
    a = _dot(h, wa_ref[...])
    b = _dot(h, wb_ref[...])
    t = (a * _sigmoid(a) * b).astype(BF16)
    y = _dot(t, wo_ref[...])
    o_ref[...] = x + 0.5 * _rms(y, post_ref[...])


def _ffn(x, pre_g, w_in, w_out, post_g, tm=512):
    n, d = x.shape
    f = w_out.shape[0]
    return pl.pallas_call(
        _ffn_kernel,
        grid=(n // tm,),
        in_specs=[pl.BlockSpec((tm, d), lambda i: (i, 0)),
                  _const_spec((1, d)),
                  pl.BlockSpec((d, f), lambda i: (0, 0), pipeline_mode=pl.Buffered(1)),
                  pl.BlockSpec((d, f), lambda i: (0, 1), pipeline_mode=pl.Buffered(1)),
                  _const_spec((f, d)),
                  _const_spec((1, d))],
        out_specs=pl.BlockSpec((tm, d), lambda i: (i, 0)),
        out_shape=jax.ShapeDtypeStruct((n, d), F32),
        compiler_params=_params("parallel"),
        name="ffn",
    )(x, pre_g[None, :], w_in, w_in, w_out, post_g[None, :])


def _inproj_kernel(x_ref, g_ref, cos_ref, sin_ref,
                   w_uc, w_up, w_qn, w_kn, w_gn, w_qs, w_ks, w_gt,
                   uc_ref, up_ref, qn_ref, kc_ref, vc_ref, kst_ref, kwt_ref, vsw_ref, gn_ref, qs_ref, kvs_ref, gt_ref):
    h = _rms(x_ref[...], g_ref[...]).astype(BF16)
    cos_t = cos_ref[...]
    sin_t = sin_ref[...]
    lane = lax.broadcasted_iota(jnp.int32, (1, LANES), 1)
    first_half = (lane & (HEAD_DIM // 2)) == 0

    def rope(seg):
        rot = jnp.where(first_half, pltpu.roll(seg, LANES - HEAD_DIM // 2, 1),
                        pltpu.roll(seg, HEAD_DIM // 2, 1))
        return seg * cos_t + rot * sin_t

    def seg(v, j):
        return v[:, j * LANES:(j + 1) * LANES]

    q_scale = HEAD_DIM ** -0.5 * LOG2E
    low_half = lane < HEAD_DIM

    def place_heads(q, out_ref, n_heads, n_groups):
        per_group = n_heads // n_groups
        for j in range(n_heads // 2):
            pair = rope(seg(q, j)) * q_scale
            swapped = pltpu.roll(pair, HEAD_DIM, 1)
            for k in range(2):
                head = 2 * j + k
                g = head // per_group
                src_tile = pair if k == g else swapped
                keep = low_half if g == 0 else jnp.logical_not(low_half)
                out_ref[head] = jnp.where(keep, src_tile, 0.0).astype(BF16)

    uc_ref[...] = _dot(h, w_uc[...]).astype(BF16)
    up_ref[...] = _dot(h, w_up[...]).astype(BF16)
    place_heads(_dot(h, w_qn[...]), qn_ref, NSA_HEADS, NSA_KV)
    kn = _dot(h, w_kn[...])
    kc_ref[...] = rope(seg(kn, 0))
    vc_ref[...] = seg(kn, 1)
    kst_ref[0] = rope(seg(kn, 2)).T.astype(BF16)
    kwt_ref[0] = rope(seg(kn, 4)).T.astype(BF16)
    vsw_ref[:, 0:LANES] = seg(kn, 3).astype(BF16)
    vsw_ref[:, LANES:2 * LANES] = seg(kn, 5).astype(BF16)
    gn_ref[...] = _dot(h, w_gn[...])
    place_heads(_dot(h, w_qs[...]), qs_ref, SWA_HEADS, SWA_KV)
    ks = _dot(h, w_ks[...])
    kvs_ref[:, 0:LANES] = rope(seg(ks, 0)).astype(BF16)
    kvs_ref[:, LANES:2 * LANES] = seg(ks, 1).astype(BF16)
    gt_ref[...] = _dot(h, w_gt[...]).astype(BF16)


def _inproj_weights(w_in):
    o = IN_OFFS
    w_uc = w_in[:, o[0]:o[1]]
    w_up = w_in[:, o[1]:o[2]]
    w_qn = w_in[:, o[2]:o[3]]
    w_kn = w_in[:, o[3]:o[4]]
    w_gn = jnp.pad(w_in[:, o[4]:o[5]], ((0, 0), (0, LANES - 3 * NSA_HEADS)))
    w_qs = w_in[:, o[5]:o[6]]
    w_ks = w_in[:, o[6]:o[7]]
    w_gt = w_in[:, o[7]:o[8]]
    return tuple(w.astype(BF16) for w in (w_uc, w_up, w_qn, w_kn, w_gn, w_qs, w_ks, w_gt))


def _inproj(x, g, cos_t, sin_t, weights, seq, tm=512):
    n, d = x.shape
    spb = seq // tm
    widths = [(2 * CONV_W, BF16, "row"), (POOL_W, BF16, "row"), (NSA_HEADS * LANES, BF16, "head"),
              (LANES, F32, "row"), (LANES, F32, "row"), (LANES, BF16, "tr"), (LANES, BF16, "tr"),
              (2 * LANES, BF16, "row"), (LANES, F32, "row"),
              (SWA_HEADS * LANES, BF16, "head"), (2 * LANES, BF16, "row"), (N_BRANCH * D_MODEL, BF16, "row")]
    specs = {"row": lambda w: pl.BlockSpec((tm, w), lambda i: (i, 0)),
             "head": lambda w: pl.BlockSpec((w // LANES, tm, LANES), lambda i: (0, i, 0)),
             "tr": lambda w: pl.BlockSpec((1, w, tm), lambda i: (i // spb, 0, i % spb))}
    shapes = {"row": lambda w: (n, w), "head": lambda w: (w // LANES, n, LANES), "tr": lambda w: (n // seq, w, seq)}
    return pl.pallas_call(
        _inproj_kernel,
        grid=(n // tm,),
        in_specs=[pl.BlockSpec((tm, d), lambda i: (i, 0)), _const_spec((1, d)),
                  pl.BlockSpec((tm, LANES), lambda i: (i, 0)), pl.BlockSpec((tm, LANES), lambda i: (i, 0))]
                 + [_const_spec(w.shape) for w in weights],
        out_specs=[specs[kind](w) for w, _, kind in widths],
        out_shape=[jax.ShapeDtypeStruct(shapes[kind](w), dt) for w, dt, kind in widths],
        compiler_params=_params("parallel"),
        name="inproj",
    )(x, g[None, :], cos_t, sin_t, *weights)


def _convpool_kernel(uc_ref, up_ref, cw_ref, cb_ref, lng_ref, lnb_ref, pw_ref, psc_ref,
                     ya_ref, yb_ref, vbuf, pbuf, cbuf, *, seq):
    t_c = CONV_T
    vbuf[0:CONV_HALO, :] = jnp.zeros((CONV_HALO, CONV_W), F32)
    pbuf[0:POOL_HALO, :] = jnp.zeros((POOL_HALO, POOL_W), F32)

    def fill(c, carry):
        r0 = pl.multiple_of(c * t_c, t_c)
        u = uc_ref[0, pl.ds(r0, t_c), :].astype(F32)
        vbuf[pl.ds(CONV_HALO + r0, t_c), :] = u[:, :CONV_W] * _sigmoid(u[:, CONV_W:])
        pbuf[pl.ds(POOL_HALO + r0, t_c), :] = up_ref[0, pl.ds(r0, t_c), :].astype(F32)
        return carry

    lax.fori_loop(0, seq // t_c, fill, 0)

    cg = POOL_W // POOL_GROUPS

    def mix(c, carry):
        r0 = pl.multiple_of(c * t_c, t_c)
        for lt in range(CONV_W // LANES):
            ls = slice(lt * LANES, (lt + 1) * LANES)
            xext = vbuf[pl.ds(r0, t_c + CONV_HALO), ls]
            acc = jnp.broadcast_to(cb_ref[:, ls], (t_c, LANES))
            for r in range(8):
                rolled = xext if r == 0 else pltpu.roll(xext, r, 0)
                for a in range(CONV_HALO // 8):
                    shift = 8 * a + r
                    if shift > CONV_K - 1:
                        continue
                    k = CONV_K - 1 - shift
                    base = CONV_HALO - 8 * a
                    acc = acc + rolled[base:base + t_c, :] * cw_ref[k:k + 1, ls]
            cbuf[:, ls] = acc
        y = cbuf[...]
        mu = jnp.mean(y, -1, keepdims=True)
        yc = y - mu
        var = jnp.mean(yc * yc, -1, keepdims=True)
        z = yc * lax.rsqrt(var + NORM_EPS) * lng_ref[...] + lnb_ref[...]
        ya_ref[0, pl.ds(r0, t_c), :] = (z * _sigmoid(z)).astype(BF16)

        pext = pbuf[pl.ds(r0, t_c + POOL_HALO), :]
        e2 = pext + pltpu.roll(pext, 1, 0)
        e4 = e2[:, cg:] + pltpu.roll(e2[:, cg:], 2, 0)
        e8 = e4[:, cg:] + pltpu.roll(e4[:, cg:], 4, 0)
        e16 = e8[:, cg:] + pltpu.roll(e8[:, cg:], 8, 0)
        tpos = r0 + lax.broadcasted_iota(jnp.int32, (t_c, 1), 0)
        for g, (w, e) in enumerate(zip(POOL_WINDOWS, (e2, e4, e8, e16))):
            cnt = jnp.minimum(tpos + 1, w).astype(F32)
            v_g = pext[POOL_HALO:, g * cg:(g + 1) * cg]
            dlt = e[POOL_HALO:, :cg] / cnt - v_g
            yg = _dot(dlt.astype(BF16), pw_ref[g]) * psc_ref[:, g * cg:(g + 1) * cg]
            yb_ref[0, pl.ds(r0, t_c), g * cg:(g + 1) * cg] = yg.astype(BF16)
        return carry

    lax.fori_loop(0, seq // t_c, mix, 0)


def _convpool(uc, up, conv_w, conv_b, ln_g, ln_b, pool_w, pool_scale):
    b, s, _ = uc.shape
    return pl.pallas_call(
        functools.partial(_convpool_kernel, seq=s),
        grid=(b,),
        in_specs=[pl.BlockSpec((1, s, 2 * CONV_W), lambda i: (i, 0, 0)),
                  pl.BlockSpec((1, s, POOL_W), lambda i: (i, 0, 0)),
                  _const_spec((CONV_K, CONV_W)), _const_spec((1, CONV_W)),
                  _const_spec((1, CONV_W)), _const_spec((1, CONV_W)),
                  _const_spec(pool_w.shape), _const_spec((1, POOL_W))],
        out_specs=[pl.BlockSpec((1, s, CONV_W), lambda i: (i, 0, 0)),
                   pl.BlockSpec((1, s, POOL_W), lambda i: (i, 0, 0))],
        out_shape=[jax.ShapeDtypeStruct((b, s, CONV_W), BF16), jax.ShapeDtypeStruct((b, s, POOL_W), BF16)],
        scratch_shapes=[pltpu.VMEM((CONV_HALO + s, CONV_W), F32), pltpu.VMEM((POOL_HALO + s, POOL_W), F32),
                        pltpu.VMEM((CONV_T, CONV_W), F32)],
        compiler_params=_params("parallel"),
        name="convpool",
    )(uc, up, conv_w, conv_b[None, :], ln_g[None, :], ln_b[None, :], pool_w.astype(BF16), pool_scale[None, :])


def _gelu_tanh(x):
    c = np.float32(np.sqrt(2.0 / np.pi))
    return x * (0.5 * (1.0 + jnp.tanh(c * (x + 0.044715 * (x * x * x)))))


def _compress_kernel(kc_ref, vc_ref, pos_ref, w1_ref, b1_ref, w2_ref, b2_ref, ko_ref, vo_ref):
    n_half = kc_ref.shape[1] // CMP_STRIDE
    for t, (src, dst) in enumerate(((kc_ref, ko_ref), (vc_ref, vo_ref))):
        x = jnp.concatenate([src[0, pl.ds(l, n_half, stride=CMP_STRIDE), :] for l in range(CMP_STRIDE)], axis=1)
        top = _dot((x + pos_ref[t, 0:1, :]).astype(BF16), w1_ref[t, 0])
        bot = _dot((x + pos_ref[t, 1:2, :]).astype(BF16), w1_ref[t, 1])
        h1 = top + pltpu.roll(bot, n_half - 1, 0) + b1_ref[t]
        out = _dot(_gelu_tanh(h1).astype(BF16), w2_ref[t]) + b2_ref[t]
        dst[0] = out.astype(BF16)


def _compress_weights(pos, w1, b1, w2, b2):
    half = CMP_BLOCK // 2
    eye = jnp.eye(NSA_KV, dtype=F32)
    pos_e = jnp.tile(pos.reshape(2, half, 1, HEAD_DIM), (1, 1, NSA_KV, 1)).reshape(2, half * NSA_KV * HEAD_DIM)
    w1r = w1.reshape(2, half, HEAD_DIM, CMP_HIDDEN)
    w1e = jnp.einsum('tldn,gh->tlgdhn', w1r, eye).reshape(2, half * NSA_KV * HEAD_DIM, NSA_KV * CMP_HIDDEN)
    b1e = jnp.tile(b1, NSA_KV)[None, :]
    w2e = jnp.einsum('nd,gh->gnhd', w2, eye).reshape(NSA_KV * CMP_HIDDEN, NSA_KV * HEAD_DIM)
    b2e = jnp.tile(b2, NSA_KV)[None, :]
    return pos_e, w1e.astype(BF16), b1e, w2e.astype(BF16), b2e


def _compress(kc, vc, kparams, vparams):
    b, s, width = kc.shape
    nh = s // CMP_STRIDE
    stk = [jnp.stack([kp, vp]) for kp, vp in zip(kparams, vparams)]
    return pl.pallas_call(
        _compress_kernel,
        grid=(b,),
        in_specs=[pl.BlockSpec((1, s, width), lambda i: (i, 0, 0))] * 2 + [_const_spec(a.shape) for a in stk],
        out_specs=[pl.BlockSpec((1, nh, LANES), lambda i: (i, 0, 0))] * 2,
        out_shape=[jax.ShapeDtypeStruct((b, nh, LANES), BF16)] * 2,
        compiler_params=_params("parallel"),
        name="nsa_compress",
    )(kc, vc, *stk)


def _stack_heads(q_ref, g, rows=None):
    rows = slice(None) if rows is None else rows
    return jnp.concatenate([q_ref[HEADS_PER_GROUP * g + r, 0, rows, :] for r in range(HEADS_PER_GROUP)], axis=0)


def _pair_groups(o_g0, o_g1, r, tq):
    lane = lax.broadcasted_iota(jnp.int32, (1, LANES), 1)
    return jnp.where(lane < HEAD_DIM, o_g0[r * tq:(r + 1) * tq], o_g1[r * tq:(r + 1) * tq])


def _with_ones(v, g):
    lane = lax.broadcasted_iota(jnp.int32, (1, LANES), 1)
    own = (lane < HEAD_DIM) if g == 0 else (lane >= HEAD_DIM)
    return jnp.where(own, v, jnp.ones_like(v))


def _pair_normalized(acc_g0, acc_g1, r, tq, extra=None):
    lane = lax.broadcasted_iota(jnp.int32, (1, LANES), 1)
    rows = slice(r * tq, (r + 1) * tq)
    a0, a1 = acc_g0[rows], acc_g1[rows]
    num = jnp.where(lane < HEAD_DIM, a0, a1)
    den = pltpu.roll(jnp.where(lane < HEAD_DIM, a1, a0), HEAD_DIM, 1)
    if extra is not None:
        den = jnp.where(lane < HEAD_DIM, den + extra[0][rows], den + extra[1][rows])
    return num / den


def _band_bias(tq, window, n_variants):
    q = np.arange(tq)[:, None]
    k = np.arange(window + tq)[None, :]
    out = np.full((n_variants, tq, window + tq), NEG_INF, np.float32)
    for v in range(n_variants):
        diff = min(v * tq, window) + q - k
        out[v][(diff >= 0) & (diff < window)] = 0.0
    return jnp.asarray(out)


def _add_tiled(s, bias, tq):
    return jnp.concatenate([s[r * tq:(r + 1) * tq] + bias for r in range(HEADS_PER_GROUP)], axis=0)


def _pair_rows(w):
    d = w.shape[1]
    return w.reshape(NSA_KV, HEADS_PER_GROUP, HEAD_DIM, d).transpose(1, 0, 2, 3).reshape(-1, d)


def _nsa_kernel(q_ref, kc_ref, vc_ref, vs_ref, vw_ref, gl_ref, ovt_ref, esel_ref, eg_ref,
                wbias_ref, kst_ref, kwt_ref, o_ref, m_ref, acc_ref, *, seq):
    tq = NSA_Q_TILE
    m_rows = HEADS_PER_GROUP * tq
    n_cmp = seq // CMP_STRIDE
    n_sel = seq // SLC_BLOCK
    k_top = min(N_SEL, n_sel)
    w_keys = NSA_WINDOW + tq
    qi = pl.program_id(1)
    q0 = pl.multiple_of(qi * tq, tq)
    row = lax.broadcasted_iota(jnp.int32, (m_rows, 1), 0)
    qpos = q0 + (row & (tq - 1))
    kcm = kc_ref[0]
    vcm = vc_ref[0]
    cidx = lax.broadcasted_iota(jnp.int32, (1, n_cmp), 1)
    cmp_valid = ((cidx * CMP_STRIDE + (CMP_BLOCK - 1)) <= qpos) & (cidx < n_cmp - 1)
    jidx = lax.broadcasted_iota(jnp.int32, (n_sel, 1), 0)
    qpos_t = q0 + lax.broadcasted_iota(jnp.int32, (1, tq), 1)
    forced = (jidx == 0) | (jidx == jnp.right_shift(qpos_t, int(np.log2(SLC_BLOCK))))
    causal_blk = (jidx * SLC_BLOCK) <= qpos_t
    kstart = pl.multiple_of(jnp.maximum(q0 - NSA_WINDOW, 0), tq)
    n_var = wbias_ref.shape[0]
    win_bias = wbias_ref[jnp.minimum(qi, n_var - 1)]
    diag_bias = wbias_ref[n_var - 1, :, NSA_WINDOW:]

    q_plain = [_stack_heads(q_ref, g) for g in range(NSA_KV)]

    o_win = []
    k_w = kwt_ref[0, :, pl.ds(kstart, w_keys)]
    v_w = vw_ref[0, pl.ds(kstart, w_keys), :]
    for g in range(NSA_KV):
        s_w = _add_tiled(_dot(q_plain[g], k_w), win_bias, tq)
        p_w = jnp.exp2(s_w - jnp.max(s_w, -1, keepdims=True)).astype(BF16)
        o_win.append(_dot(p_w, _with_ones(v_w, g)))

    q_aug, o_cmp = [], []
    for g in range(NSA_KV):
        qg = q_plain[g]

        s_c = _dot_nt(qg, kcm)
        sm = jnp.where(cmp_valid, s_c, NEG_INF)
        e = jnp.where(cmp_valid, jnp.exp2(sm - jnp.max(sm, -1, keepdims=True)), 0.0)
        p_c = e / jnp.maximum(jnp.sum(e, -1, keepdims=True), 1e-30)
        o_cmp.append(_dot(p_c.astype(BF16), vcm))

        psum = p_c[0:tq]
        for r in range(1, HEADS_PER_GROUP):
            psum = psum + p_c[r * tq:(r + 1) * tq]
        p_hi = psum.astype(BF16)
        p_lo = (psum - p_hi.astype(F32)).astype(BF16)
        ovt = ovt_ref[...]
        imp_t = (_dot_nt(ovt, p_hi) + _dot_nt(ovt, p_lo))[0:n_sel]
        score = jnp.where(causal_blk, imp_t + jnp.where(forced, FORCE_BONUS, 0.0), NEG_INF)
        sub = 8
        rank_rows = [jnp.zeros((sub, tq), jnp.int32) for _ in range(n_sel // sub)]
        for i in range(n_sel):
            ri = score[i:i + 1, :]
            for v in range(n_sel // sub):
                rows = score[v * sub:(v + 1) * sub]
                if v * sub > i:
                    beats = ri >= rows
                elif (v + 1) * sub - 1 <= i:
                    beats = ri > rows
                else:
                    later = jidx[v * sub:(v + 1) * sub] > i
                    beats = jnp.where(later, jnp.where(ri >= rows, 1, 0), jnp.where(ri > rows, 1, 0)) > 0
                rank_rows[v] = rank_rows[v] + jnp.where(beats, 1, 0)
        rank = jnp.concatenate(rank_rows, axis=0)
        bias_t = jnp.where(rank < k_top, 0.0, NEG_INF)
        bias_t = jnp.concatenate([bias_t, jnp.zeros((LANES - n_sel, tq), F32)], axis=0)
        bias = bias_t.T.astype(BF16)
        q_aug.append(jnp.concatenate([qg, jnp.concatenate([bias] * HEADS_PER_GROUP, axis=0)], axis=1))

    def sel_tile(kt, diagonal):
        k0 = pl.multiple_of(kt * tq, tq)
        k_aug = jnp.concatenate([kst_ref[0, :, pl.ds(k0, tq)], esel_ref[:, pl.ds(k0, tq)]], axis=0)
        v_t = vs_ref[0, pl.ds(k0, tq), :]
        for g in range(NSA_KV):
            m_i = m_ref[g]
            s = _dot(q_aug[g], k_aug)
            if diagonal:
                s = _add_tiled(s, diag_bias, tq)
            m_n = jnp.maximum(m_i, jnp.max(s, -1, keepdims=True))
            p = jnp.exp2(s - jnp.concatenate([m_n] * (tq // LANES), axis=1)).astype(BF16)
            acc_ref[g] = jnp.exp2(m_i - m_n) * acc_ref[g] + _dot(p, _with_ones(v_t, g))
            m_ref[g] = m_n

    m_ref[...] = jnp.full(m_ref.shape, NEG_INF, F32)
    acc_ref[...] = jnp.zeros(acc_ref.shape, F32)

    def sel_group(width):
        def body(j, first):
            for u in range(width):
                sel_tile(first + width * j + u, False)
            return first
        return body

    done = 0
    for width in (4, 2, 1):
        trips = (qi - done) // width
        lax.fori_loop(0, trips, sel_group(width), done)
        done = done + trips * width
    sel_tile(qi, True)
    o_sel = [acc_ref[g] for g in range(NSA_KV)]

    gates = _sigmoid(gl_ref[0])
    g_exp = [_split_dot(gates, eg_ref[j]) for j in range(3)]
    for r in range(HEADS_PER_GROUP):
        ls = slice(r * LANES, (r + 1) * LANES)
        y = (g_exp[0][:, ls] * _pair_groups(o_cmp[0], o_cmp[1], r, tq)
             + g_exp[1][:, ls] * _pair_normalized(o_sel[0], o_sel[1], r, tq)
             + g_exp[2][:, ls] * _pair_normalized(o_win[0], o_win[1], r, tq))
        o_ref[0, :, ls] = y.astype(BF16)


def _nsa_constants(seq):
    n_cmp = seq // CMP_STRIDE
    n_sel = seq // SLC_BLOCK
    cs = np.arange(n_cmp) * CMP_STRIDE
    ss = np.arange(n_sel) * SLC_BLOCK
    ov = (cs[:, None] <= ss[None, :] + SLC_BLOCK - 1) & (cs[:, None] + CMP_BLOCK - 1 >= ss[None, :])
    ov[n_cmp - 1, :] = False
    ovt = np.zeros((LANES, n_cmp), np.float32)
    ovt[:n_sel, :] = ov.T
    esel = np.zeros((seq, LANES), np.float32)
    esel[np.arange(seq), np.arange(seq) // SLC_BLOCK] = 1.0
    eg = np.zeros((3, LANES, NSA_HEADS * HEAD_DIM), np.float32)
    for g in range(NSA_KV):
        for r in range(HEADS_PER_GROUP):
            for j in range(3):
                c0 = r * LANES + g * HEAD_DIM
                eg[j, (g * HEADS_PER_GROUP + r) * 3 + j, c0:c0 + HEAD_DIM] = 1.0
    wbias = _band_bias(NSA_Q_TILE, NSA_WINDOW, NSA_WINDOW // NSA_Q_TILE + 1)
    return jnp.asarray(ovt, BF16), jnp.asarray(esel.T.copy(), BF16), jnp.asarray(eg, BF16), wbias


def _nsa(qn, kcm, vcm, kst, kwt, vsw, gn, consts):
    _, b, s, _ = qn.shape
    tq = NSA_Q_TILE
    assert s % tq == 0 and s >= NSA_WINDOW + tq and s // SLC_BLOCK <= LANES
    n_cmp = s // CMP_STRIDE
    ovt, esel, eg, wbias = consts
    kv_spec = lambda j: pl.BlockSpec((1, s, LANES), lambda i, t: (i, 0, j))
    return pl.pallas_call(
        functools.partial(_nsa_kernel, seq=s),
        grid=(b, s // tq),
        in_specs=[pl.BlockSpec((NSA_HEADS, 1, tq, LANES), lambda i, t: (0, i, t, 0)),
                  pl.BlockSpec((1, n_cmp, LANES), lambda i, t: (i, 0, 0)),
                  pl.BlockSpec((1, n_cmp, LANES), lambda i, t: (i, 0, 0)),
                  kv_spec(0), kv_spec(1),
                  pl.BlockSpec((1, tq, LANES), lambda i, t: (i, t, 0)),
                  _const_spec(ovt.shape), _const_spec(esel.shape), _const_spec(eg.shape),
                  _const_spec(wbias.shape),
                  pl.BlockSpec((1, LANES, s), lambda i, t: (i, 0, 0)),
                  pl.BlockSpec((1, LANES, s), lambda i, t: (i, 0, 0))],
        out_specs=pl.BlockSpec((1, tq, NSA_HEADS * HEAD_DIM), lambda i, t: (i, t, 0)),
        out_shape=jax.ShapeDtypeStruct((b, s, NSA_HEADS * HEAD_DIM), BF16),
        scratch_shapes=[pltpu.VMEM((NSA_KV, HEADS_PER_GROUP * tq, LANES), F32),
                        pltpu.VMEM((NSA_KV, HEADS_PER_GROUP * tq, LANES), F32)],
        compiler_params=_params("parallel", "arbitrary"),
        name="nsa_attention",
    )(qn, kcm, vcm, vsw, vsw, gn, ovt, esel, eg, wbias, kst, kwt)


def _swa_kernel(sink_ref, q_ref, k_ref, v_ref, bias_ref, o_ref):
    tq = SWA_Q_TILE
    w_keys = SWA_WINDOW + tq
    base = pl.program_id(1) * SWA_STEP
    sinks = [
        jnp.concatenate([jnp.full((tq, LANES), sink_ref[HEADS_PER_GROUP * g + r] * LOG2E, F32)
                         for r in range(HEADS_PER_GROUP)], axis=0)
        for g in range(SWA_KV)]
    for t in range(SWA_STEP // tq):
        q0 = base + t * tq
        kstart = pl.multiple_of(jnp.maximum(q0 - SWA_WINDOW, 0), tq)
        bias = bias_ref[1] if t > 0 else bias_ref[jnp.minimum(pl.program_id(1), 1)]
        k_t = k_ref[0, pl.ds(kstart, w_keys), :]
        v_t = v_ref[0, pl.ds(kstart, w_keys), :]
        accs, sink_terms = [], []
        for g in range(SWA_KV):
            s = _add_tiled(_dot_nt(_stack_heads(q_ref, g, slice(t * tq, (t + 1) * tq)), k_t), bias, tq)
            m = jnp.maximum(jnp.max(s, -1, keepdims=True), sinks[g])
            p = jnp.exp2(s - jnp.concatenate([m] * (w_keys // LANES), axis=1)).astype(BF16)
            accs.append(_dot(p, _with_ones(v_t, g)))
            sink_terms.append(jnp.exp2(sinks[g] - m))
        for r in range(HEADS_PER_GROUP):
            o_ref[0, t * tq:(t + 1) * tq, r * LANES:(r + 1) * LANES] = (
                _pair_normalized(accs[0], accs[1], r, tq, sink_terms).astype(BF16))


def _swa(qs, kvs, sinks):
    _, b, s, _ = qs.shape
    assert s % SWA_STEP == 0 and s >= SWA_WINDOW + SWA_Q_TILE
    bias = _band_bias(SWA_Q_TILE, SWA_WINDOW, SWA_WINDOW // SWA_Q_TILE + 1)
    return pl.pallas_call(
        _swa_kernel,
        grid=(b, s // SWA_STEP),
        in_specs=[pl.BlockSpec(memory_space=pltpu.SMEM),
                  pl.BlockSpec((SWA_HEADS, 1, SWA_STEP, LANES), lambda i, t: (0, i, t, 0)),
                  pl.BlockSpec((1, s, LANES), lambda i, t: (i, 0, 0)),
                  pl.BlockSpec((1, s, LANES), lambda i, t: (i, 0, 1)),
                  _const_spec(bias.shape)],
        out_specs=pl.BlockSpec((1, SWA_STEP, SWA_HEADS * HEAD_DIM), lambda i, t: (i, t, 0)),
        out_shape=jax.ShapeDtypeStruct((b, s, SWA_HEADS * HEAD_DIM), BF16),
        compiler_params=_params("parallel", "arbitrary"),
        name="swa_attention",
    )(sinks, qs, kvs, kvs, bias)


def _merge_kernel(x_ref, ya_ref, yb_ref, yc_ref, yd_ref, gt_ref, wb_ref, wo_ref, g_ref, o_ref):
    merged = None
    for n, y_ref in enumerate((ya_ref, yb_ref, yc_ref, yd_ref)):
        z = _dot(y_ref[...], wb_ref[n])
        gate = _sigmoid(gt_ref[:, n * D_MODEL:(n + 1) * D_MODEL].astype(F32))
        merged = gate * z if merged is None else merged + gate * z
    out = _dot(merged.astype(BF16), wo_ref[...])
    o_ref[...] = x_ref[...] + _rms(out, g_ref[...])


def _merge(x, ya, yb, yc, yd, gt, wb, wo, g, tm=1024):
    n, d = x.shape
    row = lambda w: pl.BlockSpec((tm, w), lambda i: (i, 0))
    return pl.pallas_call(
        _merge_kernel,
        grid=(n // tm,),
        in_specs=[row(d), row(BRANCH_W), row(BRANCH_W), row(BRANCH_W), row(BRANCH_W), row(N_BRANCH * d),
                  _const_spec(wb.shape), _const_spec(wo.shape), _const_spec((1, d))],
        out_specs=row(d),
        out_shape=jax.ShapeDtypeStruct((n, d), F32),
        compiler_params=_params("parallel"),
        name="merge",
    )(x, ya, yb, yc, yd, gt, wb, wo, g[None, :])


def _memkv_kernel(mem_ref, g_ref, w_ref, o_ref):
    mn = _rms(mem_ref[0], g_ref[...]).astype(BF16)
    o_ref[0] = _dot(mn, w_ref[...]).astype(BF16)


def _memkv(mem, g, w_kv):
    b, m, d = mem.shape
    return pl.pallas_call(
        _memkv_kernel,
        grid=(b,),
        in_specs=[pl.BlockSpec((1, m, d), lambda i: (i, 0, 0)), _const_spec((1, d)), _const_spec(w_kv.shape)],
        out_specs=pl.BlockSpec((1, m, 2 * d), lambda i: (i, 0, 0)),
        out_shape=jax.ShapeDtypeStruct((b, m, 2 * d), BF16),
        compiler_params=_params("parallel"),
        name="mem_kv",
    )(mem, g[None, :], w_kv)


def _xattn_kernel(x_ref, pre_ref, wq_ref, kv_ref, wo_ref, post_ref, o_ref):
    x = x_ref[0]
    h = _rms(x, pre_ref[...]).astype(BF16)
    q = (_dot(h, wq_ref[...]) * (X_HEAD_DIM ** -0.5)).astype(BF16)
    heads = []
    for hd in range(X_HEADS):
        k_h = kv_ref[0, :, hd * X_HEAD_DIM:(hd + 1) * X_HEAD_DIM]
        v_h = kv_ref[0, :, D_MODEL + hd * X_HEAD_DIM:D_MODEL + (hd + 1) * X_HEAD_DIM]
        s = _dot_nt(q[:, hd * X_HEAD_DIM:(hd + 1) * X_HEAD_DIM], k_h)
        e = jnp.exp(s - jnp.max(s, -1, keepdims=True))
        p = e / jnp.sum(e, -1, keepdims=True)
        heads.append(_dot(p.astype(BF16), v_h).astype(BF16))
    o = jnp.concatenate(heads, axis=1)
    y = _dot(o, wo_ref[...])
    o_ref[0] = x + _rms(y, post_ref[...])


def _xattn(x, pre_g, w_q, kv, w_o, post_g, tm=1024):
    b, s, d = x.shape
    m = kv.shape[1]
    return pl.pallas_call(
        _xattn_kernel,
        grid=(b, s // tm),
        in_specs=[pl.BlockSpec((1, tm, d), lambda i, t: (i, t, 0)), _const_spec((1, d)), _const_spec(w_q.shape),
                  pl.BlockSpec((1, m, 2 * d), lambda i, t: (i, 0, 0)), _const_spec(w_o.shape), _const_spec((1, d))],
        out_specs=pl.BlockSpec((1, tm, d), lambda i, t: (i, t, 0)),
        out_shape=jax.ShapeDtypeStruct((b, s, d), F32),
        compiler_params=_params("parallel", "arbitrary"),
        name="cross_attention",
    )(x, pre_g[None, :], w_q, kv, w_o, post_g[None, :])


def kernel(x, mem, positions, ffn1_pre_g, ffn1_w_in, ffn1_w_out, ffn1_post_g, mix_pre_g, w_in, conv_w, conv_b, conv_ln_g, conv_ln_b, pool_w, pool_scale, cmp_k_pos, cmp_k_w1, cmp_k_b1, cmp_k_w2, cmp_k_b2, cmp_v_pos, cmp_v_w1, cmp_v_b1, cmp_v_w2, cmp_v_b2, swa_sinks, w_branch, w_out, mix_post_g, x_pre_g, mem_g, w_xq, w_xkv, w_xo, x_post_g, ffn2_pre_g, ffn2_w_in, ffn2_w_out, ffn2_post_g):
    b, s, d = x.shape
    n = b * s
    depth = ffn1_w_in.shape[0]
    cos_t, sin_t = _rope_tables(positions)
    nsa_consts = _nsa_constants(s)
    x = x.reshape(n, d)
    for l in range(depth):
        x = _ffn(x, ffn1_pre_g[l], ffn1_w_in[l].astype(BF16), ffn1_w_out[l].astype(BF16), ffn1_post_g[l])

        uc, up, qn, kc, vc, kst, kwt, vsw, gn, qs, kvs, gt = _inproj(x, mix_pre_g[l], cos_t, sin_t,
                                                                     _inproj_weights(w_in[l]), s)
        ya, yb = _convpool(uc.reshape(b, s, -1), up.reshape(b, s, -1), conv_w[l], conv_b[l],
                           conv_ln_g[l], conv_ln_b[l], pool_w[l], pool_scale[l])
        kcm, vcm = _compress(
            kc.reshape(b, s, LANES), vc.reshape(b, s, LANES),
            _compress_weights(cmp_k_pos[l], cmp_k_w1[l], cmp_k_b1[l], cmp_k_w2[l], cmp_k_b2[l]),
            _compress_weights(cmp_v_pos[l], cmp_v_w1[l], cmp_v_b1[l], cmp_v_w2[l], cmp_v_b2[l]))
        yc = _nsa(qn.reshape(NSA_HEADS, b, s, LANES), kcm, vcm, kst, kwt, vsw.reshape(b, s, -1),
                  gn.reshape(b, s, -1), nsa_consts)
        yd = _swa(qs.reshape(SWA_HEADS, b, s, LANES), kvs.reshape(b, s, -1), swa_sinks[l])
        wb = jnp.stack([w_branch[l, 0], w_branch[l, 1], _pair_rows(w_branch[l, 2]), _pair_rows(w_branch[l, 3])])
        x = _merge(x, ya.reshape(n, -1), yb.reshape(n, -1), yc.reshape(n, -1), yd.reshape(n, -1), gt,
                   wb.astype(BF16), w_out[l].astype(BF16), mix_post_g[l])

        kv = _memkv(mem, mem_g[l], w_xkv[l].astype(BF16))
        x = _xattn(x.reshape(b, s, d), x_pre_g[l], w_xq[l].astype(BF16), kv, w_xo[l].astype(BF16),
                   x_post_g[l]).reshape(n, d)

        x = _ffn(x, ffn2_pre_g[l], ffn2_w_in[l].astype(BF16), ffn2_w_out[l].astype(BF16), ffn2_post_g[l])
    return x.reshape(b, s, d)
```

```python
import functools

import numpy as np
import jax
import jax.numpy as jnp
from jax import lax
from jax.experimental import pallas as pl
from jax.experimental.pallas import tpu as pltpu

F32 = jnp.float32
BF16 = jnp.bfloat16

D_MODEL = 1024
DEPTH = 4
HEAD_DIM = 64
ROPE_THETA = 10000.0
NORM_EPS = 1e-6
NEG_INF = -1e30
D_FF = 2816
CONV_W = 512
CONV_K = 31
POOL_W = 512
POOL_GROUPS = 4
POOL_WINDOWS = (2, 4, 8, 16)
NSA_HEADS = 8
NSA_KV = 2
CMP_BLOCK = 32
CMP_STRIDE = 16
CMP_HIDDEN = 256
SLC_BLOCK = 64
N_SEL = 16
FORCE_BONUS = 1e4
NSA_WINDOW = 512
SWA_HEADS = 8
SWA_KV = 2
SWA_WINDOW = 128
X_HEADS = 4
X_HEAD_DIM = D_MODEL // X_HEADS
N_BRANCH = 4
BRANCH_W = 512
IN_SIZES = (2 * CONV_W, POOL_W, NSA_HEADS * HEAD_DIM, 6 * NSA_KV * HEAD_DIM, 3 * NSA_HEADS,
            SWA_HEADS * HEAD_DIM, 2 * SWA_KV * HEAD_DIM, N_BRANCH * D_MODEL)
IN_OFFS = tuple(int(o) for o in np.cumsum((0,) + IN_SIZES))

LANES = 128
VMEM_LIMIT = 56 * 1024 * 1024
HEADS_PER_GROUP = NSA_HEADS // NSA_KV
NSA_Q_TILE = 256
SWA_Q_TILE = 128
SWA_STEP = 256
LOG2E = float(np.log2(np.e))
CONV_T = 512
CONV_HALO = 32
POOL_HALO = 16


def _params(*sem):
    return pltpu.CompilerParams(dimension_semantics=sem, vmem_limit_bytes=VMEM_LIMIT)


def _const_spec(shape):
    nd = len(shape)
    return pl.BlockSpec(shape, lambda *_: (0,) * nd, pipeline_mode=pl.Buffered(1))


def _rms(x, g):
    return x * lax.rsqrt(jnp.mean(x * x, -1, keepdims=True) + NORM_EPS) * g


def _sigmoid(x):
    return 0.5 * jnp.tanh(0.5 * x) + 0.5


def _dot(a, b):
    return jnp.dot(a, b, preferred_element_type=F32)


def _dot_nt(a, b):
    return lax.dot_general(a, b, (((1,), (1,)), ((), ())), preferred_element_type=F32)


def _split_dot(x, e):
    hi = x.astype(BF16)
    lo = (x - hi.astype(F32)).astype(BF16)
    return _dot(hi, e) + _dot(lo, e)


def _rope_kernel(pos_ref, inv_ref, sgn_ref, cos_ref, sin_ref):
    ang = pos_ref[...].astype(F32) * inv_ref[...]
    cos_ref[...] = jnp.cos(ang)
    sin_ref[...] = jnp.sin(ang) * sgn_ref[...]


def _rope_tables(positions):
    n = positions.size
    tm = 2048
    inv = ROPE_THETA ** (-jnp.arange(0, HEAD_DIM, 2, dtype=F32) / HEAD_DIM)
    inv = jnp.tile(inv, LANES // (HEAD_DIM // 2))[None, :]
    sgn = jnp.tile(jnp.concatenate([-jnp.ones(HEAD_DIM // 2, F32), jnp.ones(HEAD_DIM // 2, F32)]),
                   LANES // HEAD_DIM)[None, :]
    return pl.pallas_call(
        _rope_kernel,
        grid=(n // tm,),
        in_specs=[pl.BlockSpec((tm, 1), lambda i: (i, 0)),
                  _const_spec((1, LANES)), _const_spec((1, LANES))],
        out_specs=[pl.BlockSpec((tm, LANES), lambda i: (i, 0))] * 2,
        out_shape=[jax.ShapeDtypeStruct((n, LANES), F32)] * 2,
        compiler_params=_params("parallel"),
        name="rope_tables",
    )(positions.reshape(n, 1), inv, sgn)


def _ffn_kernel(x_ref, pre_ref, wa_ref, wb_ref, wo_ref, post_ref, o_ref):
    x = x_ref[...]
    h = _rms(x, pre_ref[...]).astype(BF16)
    a = _dot(h, wa_ref[...])
    b = _dot(h, wb_ref[...])
    t = (a * _sigmoid(a) * b).astype(BF16)
    y = _dot(t, wo_ref[...])
    o_ref[...] = x + 0.5 * _rms(y, post_ref[...])


def _ffn(x, pre_g, w_in, w_out, post_g, tm=512):
    n, d = x.shape
    f = w_out.shape[0]
    return pl.pallas_call(
        _ffn_kernel,
        grid=(n // tm,),
        in_specs=[pl.BlockSpec((tm, d), lambda i: (i, 0)),
                  _const_spec((1, d)),
                  pl.BlockSpec((d, f), lambda i: (0, 0), pipeline_mode=pl.Buffered(1)),
                  pl.BlockSpec((d, f), lambda i: (0, 1), pipeline_mode=pl.Buffered(1)),
                  _const_spec((f, d)),
                  _const_spec((1, d))],
        out_specs=pl.BlockSpec((tm, d), lambda i: (i, 0)),
        out_shape=jax.ShapeDtypeStruct((n, d), F32),
        compiler_params=_params("parallel"),
        name="ffn",
    )(x, pre_g[None, :], w_in, w_in, w_out, post_g[None, :])


def _inproj_kernel(x_ref, g_ref, cos_ref, sin_ref,
                   w_uc, w_up, w_qn, w_kn, w_gn, w_qs, w_ks, w_gt,
                   uc_ref, up_ref, qn_ref, kc_ref, vc_ref, kst_ref, kwt_ref, vsw_ref, gn_ref, qs_ref, kvs_ref, gt_ref):
    h = _rms(x_ref[...], g_ref[...]).astype(BF16)
    cos_t = cos_ref[...]
    sin_t = sin_ref[...]
    lane = lax.broadcasted_iota(jnp.int32, (1, LANES), 1)
    first_half = (lane & (HEAD_DIM // 2)) == 0

    def rope(seg):
        rot = jnp.where(first_half, pltpu.roll(seg, LANES - HEAD_DIM // 2, 1),
                        pltpu.roll(seg, HEAD_DIM // 2, 1))
        return seg * cos_t + rot * sin_t

    def seg(v, j):
        return v[:, j * LANES:(j + 1) * LANES]

    q_scale = HEAD_DIM ** -0.5 * LOG2E
    low_half = lane < HEAD_DIM

    def place_heads(q, out_ref, n_heads, n_groups):
        per_group = n_heads // n_groups
        for j in range(n_heads // 2):
            pair = rope(seg(q, j)) * q_scale
            swapped = pltpu.roll(pair, HEAD_DIM, 1)
            for k in range(2):
                head = 2 * j + k
                g = head // per_group
                src_tile = pair if k == g else swapped
                keep = low_half if g == 0 else jnp.logical_not(low_half)
                out_ref[head] = jnp.where(keep, src_tile, 0.0).astype(BF16)

    uc_ref[...] = _dot(h, w_uc[...]).astype(BF16)
    up_ref[...] = _dot(h, w_up[...]).astype(BF16)
    place_heads(_dot(h, w_qn[...]), qn_ref, NSA_HEADS, NSA_KV)
    kn = _dot(h, w_kn[...])
    kc_ref[...] = rope(seg(kn, 0))
    vc_ref[...] = seg(kn, 1)
    kst_ref[0] = rope(seg(kn, 2)).T.astype(BF16)
    kwt_ref[0] = rope(seg(kn, 4)).T.astype(BF16)
    vsw_ref[:, 0:LANES] = seg(kn, 3).astype(BF16)
    vsw_ref[:, LANES:2 * LANES] = seg(kn, 5).astype(BF16)
    gn_ref[...] = _dot(h, w_gn[...])
    place_heads(_dot(h, w_qs[...]), qs_ref, SWA_HEADS, SWA_KV)
    ks = _dot(h, w_ks[...])
    kvs_ref[:, 0:LANES] = rope(seg(ks, 0)).astype(BF16)
    kvs_ref[:, LANES:2 * LANES] = seg(ks, 1).astype(BF16)
    gt_ref[...] = _dot(h, w_gt[...]).astype(BF16)


def _inproj_weights(w_in):
    o = IN_OFFS
    w_uc = w_in[:, o[0]:o[1]]
    w_up = w_in[:, o[1]:o[2]]
    w_qn = w_in[:, o[2]:o[3]]
    w_kn = w_in[:, o[3]:o[4]]
    w_gn = jnp.pad(w_in[:, o[4]:o[5]], ((0, 0), (0, LANES - 3 * NSA_HEADS)))
    w_qs = w_in[:, o[5]:o[6]]
    w_ks = w_in[:, o[6]:o[7]]
    w_gt = w_in[:, o[7]:o[8]]
    return tuple(w.astype(BF16) for w in (w_uc, w_up, w_qn, w_kn, w_gn, w_qs, w_ks, w_gt))


def _inproj(x, g, cos_t, sin_t, weights, seq, tm=512):
    n, d = x.shape
    spb = seq // tm
    widths = [(2 * CONV_W, BF16, "row"), (POOL_W, BF16, "row"), (NSA_HEADS * LANES, BF16, "head"),
              (LANES, F32, "row"), (LANES, F32, "row"), (LANES, BF16, "tr"), (LANES, BF16, "tr"),
              (2 * LANES, BF16, "row"), (LANES, F32, "row"),
              (SWA_HEADS * LANES, BF16, "head"), (2 * LANES, BF16, "row"), (N_BRANCH * D_MODEL, BF16, "row")]
    specs = {"row": lambda w: pl.BlockSpec((tm, w), lambda i: (i, 0)),
             "head": lambda w: pl.BlockSpec((w // LANES, tm, LANES), lambda i: (0, i, 0)),
             "tr": lambda w: pl.BlockSpec((1, w, tm), lambda i: (i // spb, 0, i % spb))}
    shapes = {"row": lambda w: (n, w), "head": lambda w: (w // LANES, n, LANES), "tr": lambda w: (n // seq, w, seq)}
    return pl.pallas_call(
        _inproj_kernel,
        grid=(n // tm,),
        in_specs=[pl.BlockSpec((tm, d), lambda i: (i, 0)), _const_spec((1, d)),
                  pl.BlockSpec((tm, LANES), lambda i: (i, 0)), pl.BlockSpec((tm, LANES), lambda i: (i, 0))]
                 + [_const_spec(w.shape) for w in weights],
        out_specs=[specs[kind](w) for w, _, kind in widths],
        out_shape=[jax.ShapeDtypeStruct(shapes[kind](w), dt) for w, dt, kind in widths],
        compiler_params=_params("parallel"),
        name="inproj",
    )(x, g[None, :], cos_t, sin_t, *weights)


def _convpool_kernel(uc_ref, up_ref, cw_ref, cb_ref, lng_ref, lnb_ref, pw_ref, psc_ref,
                     ya_ref, yb_ref, vbuf, pbuf, cbuf, *, seq):
    t_c = CONV_T
    vbuf[0:CONV_HALO, :] = jnp.zeros((CONV_HALO, CONV_W), F32)
    pbuf[0:POOL_HALO, :] = jnp.zeros((POOL_HALO, POOL_W), F32)

    def fill(c, carry):
        r0 = pl.multiple_of(c * t_c, t_c)
        u = uc_ref[0, pl.ds(r0, t_c), :].astype(F32)
        vbuf[pl.ds(CONV_HALO + r0, t_c), :] = u[:, :CONV_W] * _sigmoid(u[:, CONV_W:])
        pbuf[pl.ds(POOL_HALO + r0, t_c), :] = up_ref[0, pl.ds(r0, t_c), :].astype(F32)
        return carry

    lax.fori_loop(0, seq // t_c, fill, 0)

    cg = POOL_W // POOL_GROUPS

    def mix(c, carry):
        r0 = pl.multiple_of(c * t_c, t_c)
        for lt in range(CONV_W // LANES):
            ls = slice(lt * LANES, (lt + 1) * LANES)
            xext = vbuf[pl.ds(r0, t_c + CONV_HALO), ls]
            acc = jnp.broadcast_to(cb_ref[:, ls], (t_c, LANES))
            for r in range(8):
                rolled = xext if r == 0 else pltpu.roll(xext, r, 0)
                for a in range(CONV_HALO // 8):
                    shift = 8 * a + r
                    if shift > CONV_K - 1:
                        continue
                    k = CONV_K - 1 - shift
                    base = CONV_HALO - 8 * a
                    acc = acc + rolled[base:base + t_c, :] * cw_ref[k:k + 1, ls]
            cbuf[:, ls] = acc
        y = cbuf[...]
        mu = jnp.mean(y, -1, keepdims=True)
        yc = y - mu
        var = jnp.mean(yc * yc, -1, keepdims=True)
        z = yc * lax.rsqrt(var + NORM_EPS) * lng_ref[...] + lnb_ref[...]
        ya_ref[0, pl.ds(r0, t_c), :] = (z * _sigmoid(z)).astype(BF16)

        pext = pbuf[pl.ds(r0, t_c + POOL_HALO), :]
        e2 = pext + pltpu.roll(pext, 1, 0)
        e4 = e2[:, cg:] + pltpu.roll(e2[:, cg:], 2, 0)
        e8 = e4[:, cg:] + pltpu.roll(e4[:, cg:], 4, 0)
        e16 = e8[:, cg:] + pltpu.roll(e8[:, cg:], 8, 0)
        tpos = r0 + lax.broadcasted_iota(jnp.int32, (t_c, 1), 0)
        for g, (w, e) in enumerate(zip(POOL_WINDOWS, (e2, e4, e8, e16))):
            cnt = jnp.minimum(tpos + 1, w).astype(F32)
            v_g = pext[POOL_HALO:, g * cg:(g + 1) * cg]
            dlt = e[POOL_HALO:, :cg] / cnt - v_g
            yg = _dot(dlt.astype(BF16), pw_ref[g]) * psc_ref[:, g * cg:(g + 1) * cg]
            yb_ref[0, pl.ds(r0, t_c), g * cg:(g + 1) * cg] = yg.astype(BF16)
        return carry

    lax.fori_loop(0, seq // t_c, mix, 0)


def _convpool(uc, up, conv_w, conv_b, ln_g, ln_b, pool_w, pool_scale):
    b, s, _ = uc.shape
    return pl.pallas_call(
        functools.partial(_convpool_kernel, seq=s),
        grid=(b,),
        in_specs=[pl.BlockSpec((1, s, 2 * CONV_W), lambda i: (i, 0, 0)),
                  pl.BlockSpec((1, s, POOL_W), lambda i: (i, 0, 0)),
                  _const_spec((CONV_K, CONV_W)), _const_spec((1, CONV_W)),
                  _const_spec((1, CONV_W)), _const_spec((1, CONV_W)),
                  _const_spec(pool_w.shape), _const_spec((1, POOL_W))],
        out_specs=[pl.BlockSpec((1, s, CONV_W), lambda i: (i, 0, 0)),
                   pl.BlockSpec((1, s, POOL_W), lambda i: (i, 0, 0))],
        out_shape=[jax.ShapeDtypeStruct((b, s, CONV_W), BF16), jax.ShapeDtypeStruct((b, s, POOL_W), BF16)],
        scratch_shapes=[pltpu.VMEM((CONV_HALO + s, CONV_W), F32), pltpu.VMEM((POOL_HALO + s, POOL_W), F32),
                        pltpu.VMEM((CONV_T, CONV_W), F32)],
        compiler_params=_params("parallel"),
        name="convpool",
    )(uc, up, conv_w, conv_b[None, :], ln_g[None, :], ln_b[None, :], pool_w.astype(BF16), pool_scale[None, :])


def _gelu_tanh(x):
    c = np.float32(np.sqrt(2.0 / np.pi))
    return x * (0.5 * (1.0 + jnp.tanh(c * (x + 0.044715 * (x * x * x)))))


def _compress_kernel(kc_ref, vc_ref, pos_ref, w1_ref, b1_ref, w2_ref, b2_ref, ko_ref, vo_ref):
    n_half = kc_ref.shape[1] // CMP_STRIDE
    for t, (src, dst) in enumerate(((kc_ref, ko_ref), (vc_ref, vo_ref))):
        x = jnp.concatenate([src[0, pl.ds(l, n_half, stride=CMP_STRIDE), :] for l in range(CMP_STRIDE)], axis=1)
        top = _dot((x + pos_ref[t, 0:1, :]).astype(BF16), w1_ref[t, 0])
        bot = _dot((x + pos_ref[t, 1:2, :]).astype(BF16), w1_ref[t, 1])
        h1 = top + pltpu.roll(bot, n_half - 1, 0) + b1_ref[t]
        out = _dot(_gelu_tanh(h1).astype(BF16), w2_ref[t]) + b2_ref[t]
        dst[0] = out.astype(BF16)


def _compress_weights(pos, w1, b1, w2, b2):
    half = CMP_BLOCK // 2
    eye = jnp.eye(NSA_KV, dtype=F32)
    pos_e = jnp.tile(pos.reshape(2, half, 1, HEAD_DIM), (1, 1, NSA_KV, 1)).reshape(2, half * NSA_KV * HEAD_DIM)
    w1r = w1.reshape(2, half, HEAD_DIM, CMP_HIDDEN)
    w1e = jnp.einsum('tldn,gh->tlgdhn', w1r, eye).reshape(2, half * NSA_KV * HEAD_DIM, NSA_KV * CMP_HIDDEN)
    b1e = jnp.tile(b1, NSA_KV)[None, :]
    w2e = jnp.einsum('nd,gh->gnhd', w2, eye).reshape(NSA_KV * CMP_HIDDEN, NSA_KV * HEAD_DIM)
    b2e = jnp.tile(b2, NSA_KV)[None, :]
    return pos_e, w1e.astype(BF16), b1e, w2e.astype(BF16), b2e


def _compress(kc, vc, kparams, vparams):
    b, s, width = kc.shape
    nh = s // CMP_STRIDE
    stk = [jnp.stack([kp, vp]) for kp, vp in zip(kparams, vparams)]
    return pl.pallas_call(
        _compress_kernel,
        grid=(b,),
        in_specs=[pl.BlockSpec((1, s, width), lambda i: (i, 0, 0))] * 2 + [_const_spec(a.shape) for a in stk],
        out_specs=[pl.BlockSpec((1, nh, LANES), lambda i: (i, 0, 0))] * 2,
        out_shape=[jax.ShapeDtypeStruct((b, nh, LANES), BF16)] * 2,
        compiler_params=_params("parallel"),
        name="nsa_compress",
    )(kc, vc, *stk)


def _stack_heads(q_ref, g, rows=None):
    rows = slice(None) if rows is None else rows
    return jnp.concatenate([q_ref[HEADS_PER_GROUP * g + r, 0, rows, :] for r in range(HEADS_PER_GROUP)], axis=0)


def _pair_groups(o_g0, o_g1, r, tq):
    lane = lax.broadcasted_iota(jnp.int32, (1, LANES), 1)
    return jnp.where(lane < HEAD_DIM, o_g0[r * tq:(r + 1) * tq], o_g1[r * tq:(r + 1) * tq])


def _with_ones(v, g):
    lane = lax.broadcasted_iota(jnp.int32, (1, LANES), 1)
    own = (lane < HEAD_DIM) if g == 0 else (lane >= HEAD_DIM)
    return jnp.where(own, v, jnp.ones_like(v))


def _pair_normalized(acc_g0, acc_g1, r, tq, extra=None):
    lane = lax.broadcasted_iota(jnp.int32, (1, LANES), 1)
    rows = slice(r * tq, (r + 1) * tq)
    a0, a1 = acc_g0[rows], acc_g1[rows]
    num = jnp.where(lane < HEAD_DIM, a0, a1)
    den = pltpu.roll(jnp.where(lane < HEAD_DIM, a1, a0), HEAD_DIM, 1)
    if extra is not None:
        den = jnp.where(lane < HEAD_DIM, den + extra[0][rows], den + extra[1][rows])
    return num / den


def _band_bias(tq, window, n_variants):
    q = np.arange(tq)[:, None]
    k = np.arange(window + tq)[None, :]
    out = np.full((n_variants, tq, window + tq), NEG_INF, np.float32)
    for v in range(n_variants):
        diff = min(v * tq, window) + q - k
        out[v][(diff >= 0) & (diff < window)] = 0.0
    return jnp.asarray(out)


def _add_tiled(s, bias, tq):
    return jnp.concatenate([s[r * tq:(r + 1) * tq] + bias for r in range(HEADS_PER_GROUP)], axis=0)


def _pair_rows(w):
    d = w.shape[1]
    return w.reshape(NSA_KV, HEADS_PER_GROUP, HEAD_DIM, d).transpose(1, 0, 2, 3).reshape(-1, d)


def _nsa_kernel(q_ref, kc_ref, vc_ref, vs_ref, vw_ref, gl_ref, ovt_ref, esel_ref, eg_ref,
                wbias_ref, kst_ref, kwt_ref, o_ref, m_ref, acc_ref, *, seq):
    tq = NSA_Q_TILE
    m_rows = HEADS_PER_GROUP * tq
    n_cmp = seq // CMP_STRIDE
    n_sel = seq // SLC_BLOCK
    k_top = min(N_SEL, n_sel)
    w_keys = NSA_WINDOW + tq
    qi = pl.program_id(1)
    q0 = pl.multiple_of(qi * tq, tq)
    row = lax.broadcasted_iota(jnp.int32, (m_rows, 1), 0)
    qpos = q0 + (row & (tq - 1))
    kcm = kc_ref[0]
    vcm = vc_ref[0]
    cidx = lax.broadcasted_iota(jnp.int32, (1, n_cmp), 1)
    cmp_valid = ((cidx * CMP_STRIDE + (CMP_BLOCK - 1)) <= qpos) & (cidx < n_cmp - 1)
    jidx = lax.broadcasted_iota(jnp.int32, (n_sel, 1), 0)
    qpos_t = q0 + lax.broadcasted_iota(jnp.int32, (1, tq), 1)
    forced = (jidx == 0) | (jidx == jnp.right_shift(qpos_t, int(np.log2(SLC_BLOCK))))
    causal_blk = (jidx * SLC_BLOCK) <= qpos_t
    kstart = pl.multiple_of(jnp.maximum(q0 - NSA_WINDOW, 0), tq)
    n_var = wbias_ref.shape[0]
    win_bias = wbias_ref[jnp.minimum(qi, n_var - 1)]
    diag_bias = wbias_ref[n_var - 1, :, NSA_WINDOW:]

    q_plain = [_stack_heads(q_ref, g) for g in range(NSA_KV)]

    o_win = []
    k_w = kwt_ref[0, :, pl.ds(kstart, w_keys)]
    v_w = vw_ref[0, pl.ds(kstart, w_keys), :]
    for g in range(NSA_KV):
        s_w = _add_tiled(_dot(q_plain[g], k_w), win_bias, tq)
        p_w = jnp.exp2(s_w - jnp.max(s_w, -1, keepdims=True)).astype(BF16)
        o_win.append(_dot(p_w, _with_ones(v_w, g)))

    q_aug, o_cmp = [], []
    for g in range(NSA_KV):
        qg = q_plain[g]

        s_c = _dot_nt(qg, kcm)
        sm = jnp.where(cmp_valid, s_c, NEG_INF)
        e = jnp.where(cmp_valid, jnp.exp2(sm - jnp.max(sm, -1, keepdims=True)), 0.0)
        p_c = e / jnp.maximum(jnp.sum(e, -1, keepdims=True), 1e-30)
        o_cmp.append(_dot(p_c.astype(BF16), vcm))

        psum = p_c[0:tq]
        for r in range(1, HEADS_PER_GROUP):
            psum = psum + p_c[r * tq:(r + 1) * tq]
        p_hi = psum.astype(BF16)
        p_lo = (psum - p_hi.astype(F32)).astype(BF16)
        ovt = ovt_ref[...]
        imp_t = (_dot_nt(ovt, p_hi) + _dot_nt(ovt, p_lo))[0:n_sel]
        score = jnp.where(causal_blk, imp_t + jnp.where(forced, FORCE_BONUS, 0.0), NEG_INF)
        sub = 8
        rank_rows = [jnp.zeros((sub, tq), jnp.int32) for _ in range(n_sel // sub)]
        for i in range(n_sel):
            ri = score[i:i + 1, :]
            for v in range(n_sel // sub):
                rows = score[v * sub:(v + 1) * sub]
                if v * sub > i:
                    beats = ri >= rows
                elif (v + 1) * sub - 1 <= i:
                    beats = ri > rows
                else:
                    later = jidx[v * sub:(v + 1) * sub] > i
                    beats = jnp.where(later, jnp.where(ri >= rows, 1, 0), jnp.where(ri > rows, 1, 0)) > 0
                rank_rows[v] = rank_rows[v] + jnp.where(beats, 1, 0)
        rank = jnp.concatenate(rank_rows, axis=0)
        bias_t = jnp.where(rank < k_top, 0.0, NEG_INF)
        bias_t = jnp.concatenate([bias_t, jnp.zeros((LANES - n_sel, tq), F32)], axis=0)
        bias = bias_t.T.astype(BF16)
        q_aug.append(jnp.concatenate([qg, jnp.concatenate([bias] * HEADS_PER_GROUP, axis=0)], axis=1))

    def sel_tile(kt, diagonal):
        k0 = pl.multiple_of(kt * tq, tq)
        k_aug = jnp.concatenate([kst_ref[0, :, pl.ds(k0, tq)], esel_ref[:, pl.ds(k0, tq)]], axis=0)
        v_t = vs_ref[0, pl.ds(k0, tq), :]
        for g in range(NSA_KV):
            m_i = m_ref[g]
            s = _dot(q_aug[g], k_aug)
            if diagonal:
                s = _add_tiled(s, diag_bias, tq)
            m_n = jnp.maximum(m_i, jnp.max(s, -1, keepdims=True))
            p = jnp.exp2(s - jnp.concatenate([m_n] * (tq // LANES), axis=1)).astype(BF16)
            acc_ref[g] = jnp.exp2(m_i - m_n) * acc_ref[g] + _dot(p, _with_ones(v_t, g))
            m_ref[g] = m_n

    m_ref[...] = jnp.full(m_ref.shape, NEG_INF, F32)
    acc_ref[...] = jnp.zeros(acc_ref.shape, F32)

    def sel_group(width):
        def body(j, first):
            for u in range(width):
                sel_tile(first + width * j + u, False)
            return first
        return body

    done = 0
    for width in (4, 2, 1):
        trips = (qi - done) // width
        lax.fori_loop(0, trips, sel_group(width), done)
        done = done + trips * width
    sel_tile(qi, True)
    o_sel = [acc_ref[g] for g in range(NSA_KV)]

    gates = _sigmoid(gl_ref[0])
    g_exp = [_split_dot(gates, eg_ref[j]) for j in range(3)]
    for r in range(HEADS_PER_GROUP):
        ls = slice(r * LANES, (r + 1) * LANES)
        y = (g_exp[0][:, ls] * _pair_groups(o_cmp[0], o_cmp[1], r, tq)
             + g_exp[1][:, ls] * _pair_normalized(o_sel[0], o_sel[1], r, tq)
             + g_exp[2][:, ls] * _pair_normalized(o_win[0], o_win[1], r, tq))
        o_ref[0, :, ls] = y.astype(BF16)


def _nsa_constants(seq):
    n_cmp = seq // CMP_STRIDE
    n_sel = seq // SLC_BLOCK
    cs = np.arange(n_cmp) * CMP_STRIDE
    ss = np.arange(n_sel) * SLC_BLOCK
    ov = (cs[:, None] <= ss[None, :] + SLC_BLOCK - 1) & (cs[:, None] + CMP_BLOCK - 1 >= ss[None, :])
    ov[n_cmp - 1, :] = False
    ovt = np.zeros((LANES, n_cmp), np.float32)
    ovt[:n_sel, :] = ov.T
    esel = np.zeros((seq, LANES), np.float32)
    esel[np.arange(seq), np.arange(seq) // SLC_BLOCK] = 1.0
    eg = np.zeros((3, LANES, NSA_HEADS * HEAD_DIM), np.float32)
    for g in range(NSA_KV):
        for r in range(HEADS_PER_GROUP):
            for j in range(3):
                c0 = r * LANES + g * HEAD_DIM
                eg[j, (g * HEADS_PER_GROUP + r) * 3 + j, c0:c0 + HEAD_DIM] = 1.0
    wbias = _band_bias(NSA_Q_TILE, NSA_WINDOW, NSA_WINDOW // NSA_Q_TILE + 1)
    return jnp.asarray(ovt, BF16), jnp.asarray(esel.T.copy(), BF16), jnp.asarray(eg, BF16), wbias


def _nsa(qn, kcm, vcm, kst, kwt, vsw, gn, consts):
    _, b, s, _ = qn.shape
    tq = NSA_Q_TILE
    assert s % tq == 0 and s >= NSA_WINDOW + tq and s // SLC_BLOCK <= LANES
    n_cmp = s // CMP_STRIDE
    ovt, esel, eg, wbias = consts
    kv_spec = lambda j: pl.BlockSpec((1, s, LANES), lambda i, t: (i, 0, j))
    return pl.pallas_call(
        functools.partial(_nsa_kernel, seq=s),
        grid=(b, s // tq),
        in_specs=[pl.BlockSpec((NSA_HEADS, 1, tq, LANES), lambda i, t: (0, i, t, 0)),
                  pl.BlockSpec((1, n_cmp, LANES), lambda i, t: (i, 0, 0)),
                  pl.BlockSpec((1, n_cmp, LANES), lambda i, t: (i, 0, 0)),
                  kv_spec(0), kv_spec(1),
                  pl.BlockSpec((1, tq, LANES), lambda i, t: (i, t, 0)),
                  _const_spec(ovt.shape), _const_spec(esel.shape), _const_spec(eg.shape),
                  _const_spec(wbias.shape),
                  pl.BlockSpec((1, LANES, s), lambda i, t: (i, 0, 0)),
                  pl.BlockSpec((1, LANES, s), lambda i, t: (i, 0, 0))],
        out_specs=pl.BlockSpec((1, tq, NSA_HEADS * HEAD_DIM), lambda i, t: (i, t, 0)),
        out_shape=jax.ShapeDtypeStruct((b, s, NSA_HEADS * HEAD_DIM), BF16),
        scratch_shapes=[pltpu.VMEM((NSA_KV, HEADS_PER_GROUP * tq, LANES), F32),
                        pltpu.VMEM((NSA_KV, HEADS_PER_GROUP * tq, LANES), F32)],
        compiler_params=_params("parallel", "arbitrary"),
        name="nsa_attention",
    )(qn, kcm, vcm, vsw, vsw, gn, ovt, esel, eg, wbias, kst, kwt)


def _swa_kernel(sink_ref, q_ref, k_ref, v_ref, bias_ref, o_ref):
    tq = SWA_Q_TILE
    w_keys = SWA_WINDOW + tq
    base = pl.program_id(1) * SWA_STEP
    sinks = [
        jnp.concatenate([jnp.full((tq, LANES), sink_ref[HEADS_PER_GROUP * g + r] * LOG2E, F32)
                         for r in range(HEADS_PER_GROUP)], axis=0)
        for g in range(SWA_KV)]
    for t in range(SWA_STEP // tq):
        q0 = base + t * tq
        kstart = pl.multiple_of(jnp.maximum(q0 - SWA_WINDOW, 0), tq)
        bias = bias_ref[1] if t > 0 else bias_ref[jnp.minimum(pl.program_id(1), 1)]
        k_t = k_ref[0, pl.ds(kstart, w_keys), :]
        v_t = v_ref[0, pl.ds(kstart, w_keys), :]
        accs, sink_terms = [], []
        for g in range(SWA_KV):
            s = _add_tiled(_dot_nt(_stack_heads(q_ref, g, slice(t * tq, (t + 1) * tq)), k_t), bias, tq)
            m = jnp.maximum(jnp.max(s, -1, keepdims=True), sinks[g])
            p = jnp.exp2(s - jnp.concatenate([m] * (w_keys // LANES), axis=1)).astype(BF16)
            accs.append(_dot(p, _with_ones(v_t, g)))
            sink_terms.append(jnp.exp2(sinks[g] - m))
        for r in range(HEADS_PER_GROUP):
            o_ref[0, t * tq:(t + 1) * tq, r * LANES:(r + 1) * LANES] = (
                _pair_normalized(accs[0], accs[1], r, tq, sink_terms).astype(BF16))


def _swa(qs, kvs, sinks):
    _, b, s, _ = qs.shape
    assert s % SWA_STEP == 0 and s >= SWA_WINDOW + SWA_Q_TILE
    bias = _band_bias(SWA_Q_TILE, SWA_WINDOW, SWA_WINDOW // SWA_Q_TILE + 1)
    return pl.pallas_call(
        _swa_kernel,
        grid=(b, s // SWA_STEP),
        in_specs=[pl.BlockSpec(memory_space=pltpu.SMEM),
                  pl.BlockSpec((SWA_HEADS, 1, SWA_STEP, LANES), lambda i, t: (0, i, t, 0)),
                  pl.BlockSpec((1, s, LANES), lambda i, t: (i, 0, 0)),
                  pl.BlockSpec((1, s, LANES), lambda i, t: (i, 0, 1)),
                  _const_spec(bias.shape)],
        out_specs=pl.BlockSpec((1, SWA_STEP, SWA_HEADS * HEAD_DIM), lambda i, t: (i, t, 0)),
        out_shape=jax.ShapeDtypeStruct((b, s, SWA_HEADS * HEAD_DIM), BF16),
        compiler_params=_params("parallel", "arbitrary"),
        name="swa_attention",
    )(sinks, qs, kvs, kvs, bias)


def _merge_kernel(x_ref, ya_ref, yb_ref, yc_ref, yd_ref, gt_ref, wb_ref, wo_ref, g_ref, o_ref):
    merged = None
    for n, y_ref in enumerate((ya_ref, yb_ref, yc_ref, yd_ref)):
        z = _dot(y_ref[...], wb_ref[n])
        gate = _sigmoid(gt_ref[:, n * D_MODEL:(n + 1) * D_MODEL].astype(F32))
        merged = gate * z if merged is None else merged + gate * z
    out = _dot(merged.astype(BF16), wo_ref[...])
    o_ref[...] = x_ref[...] + _rms(out, g_ref[...])


def _merge(x, ya, yb, yc, yd, gt, wb, wo, g, tm=1024):
    n, d = x.shape
    row = lambda w: pl.BlockSpec((tm, w), lambda i: (i, 0))
    return pl.pallas_call(
        _merge_kernel,
        grid=(n // tm,),
        in_specs=[row(d), row(BRANCH_W), row(BRANCH_W), row(BRANCH_W), row(BRANCH_W), row(N_BRANCH * d),
                  _const_spec(wb.shape), _const_spec(wo.shape), _const_spec((1, d))],
        out_specs=row(d),
        out_shape=jax.ShapeDtypeStruct((n, d), F32),
        compiler_params=_params("parallel"),
        name="merge",
    )(x, ya, yb, yc, yd, gt, wb, wo, g[None, :])


def _memkv_kernel(mem_ref, g_ref, w_ref, o_ref):
    mn = _rms(mem_ref[0], g_ref[...]).astype(BF16)
    o_ref[0] = _dot(mn, w_ref[...]).astype(BF16)


def _memkv(mem, g, w_kv):
    b, m, d = mem.shape
    return pl.pallas_call(
        _memkv_kernel,
        grid=(b,),
        in_specs=[pl.BlockSpec((1, m, d), lambda i: (i, 0, 0)), _const_spec((1, d)), _const_spec(w_kv.shape)],
        out_specs=pl.BlockSpec((1, m, 2 * d), lambda i: (i, 0, 0)),
        out_shape=jax.ShapeDtypeStruct((b, m, 2 * d), BF16),
        compiler_params=_params("parallel"),
        name="mem_kv",
    )(mem, g[None, :], w_kv)


def _xattn_kernel(x_ref, pre_ref, wq_ref, kv_ref, wo_ref, post_ref, o_ref):
    x = x_ref[0]
    h = _rms(x, pre_ref[...]).astype(BF16)
    q = (_dot(h, wq_ref[...]) * (X_HEAD_DIM ** -0.5)).astype(BF16)
    heads = []
    for hd in range(X_HEADS):
        k_h = kv_ref[0, :, hd * X_HEAD_DIM:(hd + 1) * X_HEAD_DIM]
        v_h = kv_ref[0, :, D_MODEL + hd * X_HEAD_DIM:D_MODEL + (hd + 1) * X_HEAD_DIM]
        s = _dot_nt(q[:, hd * X_HEAD_DIM:(hd + 1) * X_HEAD_DIM], k_h)
        e = jnp.exp(s - jnp.max(s, -1, keepdims=True))
        p = e / jnp.sum(e, -1, keepdims=True)
        heads.append(_dot(p.astype(BF16), v_h).astype(BF16))
    o = jnp.concatenate(heads, axis=1)
    y = _dot(o, wo_ref[...])
    o_ref[0] = x + _rms(y, post_ref[...])


def _xattn(x, pre_g, w_q, kv, w_o, post_g, tm=2048):
    b, s, d = x.shape
    m = kv.shape[1]
    return pl.pallas_call(
        _xattn_kernel,
        grid=(b, s // tm),
        in_specs=[pl.BlockSpec((1, tm, d), lambda i, t: (i, t, 0)), _const_spec((1, d)), _const_spec(w_q.shape),
                  pl.BlockSpec((1, m, 2 * d), lambda i, t: (i, 0, 0)), _const_spec(w_o.shape), _const_spec((1, d))],
        out_specs=pl.BlockSpec((1, tm, d), lambda i, t: (i, t, 0)),
        out_shape=jax.ShapeDtypeStruct((b, s, d), F32),
        compiler_params=_params("parallel", "arbitrary"),
        name="cross_attention",
    )(x, pre_g[None, :], w_q, kv, w_o, post_g[None, :])


def kernel(x, mem, positions, ffn1_pre_g, ffn1_w_in, ffn1_w_out, ffn1_post_g, mix_pre_g, w_in, conv_w, conv_b, conv_ln_g, conv_ln_b, pool_w, pool_scale, cmp_k_pos, cmp_k_w1, cmp_k_b1, cmp_k_w2, cmp_k_b2, cmp_v_pos, cmp_v_w1, cmp_v_b1, cmp_v_w2, cmp_v_b2, swa_sinks, w_branch, w_out, mix_post_g, x_pre_g, mem_g, w_xq, w_xkv, w_xo, x_post_g, ffn2_pre_g, ffn2_w_in, ffn2_w_out, ffn2_post_g):
    b, s, d = x.shape
    n = b * s
    depth = ffn1_w_in.shape[0]
    cos_t, sin_t = _rope_tables(positions)
    nsa_consts = _nsa_constants(s)
    x = x.reshape(n, d)
    for l in range(depth):
        x = _ffn(x, ffn1_pre_g[l], ffn1_w_in[l].astype(BF16), ffn1_w_out[l].astype(BF16), ffn1_post_g[l])

        uc, up, qn, kc, vc, kst, kwt, vsw, gn, qs, kvs, gt = _inproj(x, mix_pre_g[l], cos_t, sin_t,
                                                                     _inproj_weights(w_in[l]), s)
        ya, yb = _convpool(uc.reshape(b, s, -1), up.reshape(b, s, -1), conv_w[l], conv_b[l],
                           conv_ln_g[l], conv_ln_b[l], pool_w[l], pool_scale[l])
        kcm, vcm = _compress(
            kc.reshape(b, s, LANES), vc.reshape(b, s, LANES),
            _compress_weights(cmp_k_pos[l], cmp_k_w1[l], cmp_k_b1[l], cmp_k_w2[l], cmp_k_b2[l]),
            _compress_weights(cmp_v_pos[l], cmp_v_w1[l], cmp_v_b1[l], cmp_v_w2[l], cmp_v_b2[l]))
        yc = _nsa(qn.reshape(NSA_HEADS, b, s, LANES), kcm, vcm, kst, kwt, vsw.reshape(b, s, -1),
                  gn.reshape(b, s, -1), nsa_consts)
        yd = _swa(qs.reshape(SWA_HEADS, b, s, LANES), kvs.reshape(b, s, -1), swa_sinks[l])
        wb = jnp.stack([w_branch[l, 0], w_branch[l, 1], _pair_rows(w_branch[l, 2]), _pair_rows(w_branch[l, 3])])
        x = _merge(x, ya.reshape(n, -1), yb.reshape(n, -1), yc.reshape(n, -1), yd.reshape(n, -1), gt,
                   wb.astype(BF16), w_out[l].astype(BF16), mix_post_g[l])

        kv = _memkv(mem, mem_g[l], w_xkv[l].astype(BF16))
        x = _xattn(x.reshape(b, s, d), x_pre_g[l], w_xq[l].astype(BF16), kv, w_xo[l].astype(BF16),
                   x_post_g[l]).reshape(n, d)

        x = _ffn(x, ffn2_pre_g[l], ffn2_w_in[l].astype(BF16), ffn2_w_out[l].astype(BF16), ffn2_post_g[l])
    return x.reshape(b, s, d)
```
